```python
import math
import jax, jax.numpy as jnp
from jax import lax
import numpy as np

D_MODEL = 2048
BATCH = 2
SEQ = 16384
DEPTH = 2

HEAD_DIM = 128
A_GROUPS = ((128, 1), (512, 4), (2048, 16))
A_HEADS_PER_GROUP = 4
A_HEADS = A_HEADS_PER_GROUP * len(A_GROUPS)
A_BLOCK = 64
A_QKV_W = A_HEADS * HEAD_DIM
A_OUT_W = A_HEADS_PER_GROUP * HEAD_DIM
B_Q_HEADS = 8
B_KV_HEADS = 2
B_HALF = 128
B_BLOCK = 128
B_Q_W = B_Q_HEADS * HEAD_DIM
B_KV_W = B_KV_HEADS * HEAD_DIM
NUM_BUCKETS = 32
MAX_DISTANCE = 1024
N_BIAS_HEADS = A_HEADS + B_Q_HEADS
SPLITS = (A_QKV_W, A_QKV_W, A_QKV_W, B_Q_W, B_KV_W, B_KV_W, D_MODEL, D_MODEL)
C_IN = sum(SPLITS)
D_FF = -(-8 * D_MODEL // (3 * 256)) * 256
DEEPNORM_ALPHA = (2 * DEPTH) ** 0.25
DEEPNORM_BETA = (8 * DEPTH) ** -0.25
LN_EPS = 1e-5
NEG_INF = -1e30

kernel_name = 'hybrid_dilated_window_gqa_encoder'


def t5_bucket(rel):
    half = NUM_BUCKETS // 2
    max_exact = half // 2
    n = np.abs(rel)
    scaled = np.log(np.maximum(n, 1) / max_exact) / math.log(MAX_DISTANCE / max_exact)
    large = np.minimum(max_exact + (scaled * (half - max_exact)).astype(np.int64), half - 1)
    return (np.where(rel > 0, half, 0) + np.where(n < max_exact, n, large)).astype(np.int32)


def band_offsets(blk):
    qi = np.arange(blk)[:, None]
    kj = np.arange(3 * blk)[None, :]
    return kj - blk - qi


def banded_attention(q, k, v, bias, half, blk, sink=None):
    n_, hk, r, seq_len, dh = q.shape
    nb = -(-seq_len // blk)
    lp = nb * blk
    qb = jnp.pad(q, ((0, 0), (0, 0), (0, 0), (0, lp - seq_len), (0, 0)))
    qb = qb.reshape(n_, hk, r, nb, blk, dh).astype(jnp.float32)

    def windows(t):
        tp = jnp.pad(t, ((0, 0), (0, 0), (blk, lp - seq_len + blk), (0, 0)))
        tp = tp.reshape(n_, hk, nb + 2, blk, dh)
        return jnp.concatenate([tp[:, :, :nb], tp[:, :, 1:nb + 1], tp[:, :, 2:]], axis=3).astype(jnp.float32)

    kw, vw = windows(k), windows(v)
    rel = band_offsets(blk)
    key_pos = (np.arange(nb)[:, None] - 1) * blk + np.arange(3 * blk)[None, :]
    mask = (np.abs(rel) <= half)[None] & ((key_pos >= 0) & (key_pos < seq_len))[:, None, :]
    s = jnp.einsum('nhrcid,nhcjd->nhrcij', qb, kw) * (dh ** -0.5) + bias[:, :, None]
    s = jnp.where(mask, s, NEG_INF)
    m = s.max(-1)
    if sink is not None:
        sk = sink.astype(jnp.float32)[:, :, None, None]
        m = jnp.maximum(m, sk)
    p = jnp.exp(s - m[..., None])
    den = p.sum(-1)
    if sink is not None:
        den = den + jnp.exp(sk - m)
    o = jnp.einsum('nhrcij,nhcjd->nhrcid', p, vw) / den[..., None]
    o = o.reshape(n_, hk, r, lp, dh)[..., :seq_len, :]
    lse = (m + jnp.log(den)).reshape(n_, hk, r, lp)[..., :seq_len]
    return o, lse


def dilated_mixture(q, k, v, rel_bias):
    bt, s_len, _ = q.shape
    hg = A_HEADS_PER_GROUP
    shp = (bt, s_len, len(A_GROUPS), hg, HEAD_DIM)
    q, k, v = q.reshape(shp), k.reshape(shp), v.reshape(shp)
    rel = band_offsets(A_BLOCK)
    outs, lses = [], []
    for g, (window, dil) in enumerate(A_GROUPS):
        half = window // (2 * dil)
        sub_len = s_len // dil

        def strided(t):
            t = t[:, :, g].reshape(bt, sub_len, dil, hg, HEAD_DIM)
            return t.transpose(0, 2, 3, 1, 4).reshape(bt * dil, hg, sub_len, HEAD_DIM)

        cols = slice(g * hg, (g + 1) * hg)
        bias = rel_bias[t5_bucket(rel * dil)][..., cols]
        bias = bias.transpose(2, 0, 1)[:, None].astype(jnp.float32)
        o, lse = banded_attention(strided(q)[:, :, None], strided(k), strided(v), bias, half, A_BLOCK)
        o = o[:, :, 0].reshape(bt, dil, hg, sub_len, HEAD_DIM).transpose(0, 3, 1, 2, 4)
        outs.append(o.reshape(bt, s_len, hg, HEAD_DIM))
        lse = lse[:, :, 0].reshape(bt, dil, hg, sub_len).transpose(0, 3, 1, 2)
        lses.append(lse.reshape(bt, s_len, hg))
    w = jax.nn.softmax(jnp.stack(lses), axis=0)
    o = jnp.einsum('gbsh,gbshd->bshd', w, jnp.stack(outs))
    return o.reshape(bt, s_len, A_OUT_W)


def windowed_gqa(q, k, v, rel_bias, sink):
    bt, s_len, _ = q.shape
    rep = B_Q_HEADS // B_KV_HEADS
    q = q.reshape(bt, s_len, B_KV_HEADS, rep, HEAD_DIM).transpose(0, 2, 3, 1, 4)
    k = k.reshape(bt, s_len, B_KV_HEADS, HEAD_DIM).transpose(0, 2, 1, 3)
    v = v.reshape(bt, s_len, B_KV_HEADS, HEAD_DIM).transpose(0, 2, 1, 3)
    bias = rel_bias[t5_bucket(band_offsets(B_BLOCK))][..., A_HEADS:]
    bias = bias.transpose(2, 0, 1).reshape(B_KV_HEADS, rep, B_BLOCK, 3 * B_BLOCK).astype(jnp.float32)
    o, _ = banded_attention(q, k, v, bias, B_HALF, B_BLOCK, sink.reshape(B_KV_HEADS, rep))
    return o.transpose(0, 3, 1, 2, 4).reshape(bt, s_len, B_Q_W)


def layer_norm(x, g, b):
    xf = x.astype(jnp.float32)
    mu = xf.mean(-1, keepdims=True)
    var = jnp.square(xf - mu).mean(-1, keepdims=True)
    return ((xf - mu) * lax.rsqrt(var + LN_EPS) * g + b).astype(x.dtype)


def hybrid_layer(x, rel_bias, w_in, sink, w_pa, w_pb, w_out, ln1_g, ln1_b, w_up, w_down, ln2_g, ln2_b):
    dt = x.dtype
    split_points = [int(c) for c in np.cumsum(SPLITS)[:-1]]
    proj = jnp.einsum('bsd,dc->bsc', x, w_in)
    qa, ka, va, qb, kb, vb, ga, gb = jnp.split(proj, split_points, axis=-1)
    ya = dilated_mixture(qa, ka, va, rel_bias).astype(dt) @ w_pa
    yb = windowed_gqa(qb, kb, vb, rel_bias, sink).astype(dt) @ w_pb
    merged = jax.nn.sigmoid(ga) * ya + jax.nn.sigmoid(gb) * yb
    x = layer_norm(DEEPNORM_ALPHA * x + merged @ w_out, ln1_g, ln1_b)
    gate, up = jnp.split(x @ w_up, 2, axis=-1)
    x = layer_norm(DEEPNORM_ALPHA * x + (jax.nn.silu(gate) * up) @ w_down, ln2_g, ln2_b)
    return x


def setup_inputs(seed: int = 0) -> dict:
    key = jax.random.key(seed)
    ks = jax.random.split(key, 13)
    beta = DEEPNORM_BETA
    nrm = jax.random.normal
    col_scale = np.ones((C_IN,), np.float32)
    col_scale[2 * A_QKV_W:3 * A_QKV_W] = beta
    vb0 = 3 * A_QKV_W + B_Q_W + B_KV_W
    col_scale[vb0:vb0 + B_KV_W] = beta
    x = nrm(ks[0], (BATCH, SEQ, D_MODEL), jnp.float32)
    rel_bias = 0.5 * nrm(ks[1], (NUM_BUCKETS, N_BIAS_HEADS), jnp.float32)
    w_in = nrm(ks[2], (DEPTH, D_MODEL, C_IN), jnp.float32) * (D_MODEL ** -0.5) * jnp.asarray(col_scale)
    sink = 0.5 * nrm(ks[3], (DEPTH, B_Q_HEADS), jnp.float32)
    w_pa = nrm(ks[4], (DEPTH, A_OUT_W, D_MODEL), jnp.float32) * (A_OUT_W ** -0.5) * beta
    w_pb = nrm(ks[5], (DEPTH, B_Q_W, D_MODEL), jnp.float32) * (B_Q_W ** -0.5) * beta
    w_out = nrm(ks[6], (DEPTH, D_MODEL, D_MODEL), jnp.float32) * (D_MODEL ** -0.5) * beta
    ln1_g = 1.0 + 0.02 * nrm(ks[7], (DEPTH, D_MODEL), jnp.float32)
    ln1_b = 0.02 * nrm(ks[8], (DEPTH, D_MODEL), jnp.float32)
    w_up = nrm(ks[9], (DEPTH, D_MODEL, 2 * D_FF), jnp.float32) * (D_MODEL ** -0.5) * beta
    w_down = nrm(ks[10], (DEPTH, D_FF, D_MODEL), jnp.float32) * (D_FF ** -0.5) * beta
    ln2_g = 1.0 + 0.02 * nrm(ks[11], (DEPTH, D_MODEL), jnp.float32)
    ln2_b = 0.02 * nrm(ks[12], (DEPTH, D_MODEL), jnp.float32)
    return {'x': x, 'rel_bias': rel_bias, 'w_in': w_in, 'sink': sink, 'w_pa': w_pa, 'w_pb': w_pb,
            'w_out': w_out, 'ln1_g': ln1_g, 'ln1_b': ln1_b, 'w_up': w_up, 'w_down': w_down,
            'ln2_g': ln2_g, 'ln2_b': ln2_b}


def reference(x, rel_bias, w_in, sink, w_pa, w_pb, w_out, ln1_g, ln1_b, w_up, w_down, ln2_g, ln2_b):
    for l in range(DEPTH):
        x = hybrid_layer(x, rel_bias, w_in[l], sink[l], w_pa[l], w_pb[l], w_out[l],
                         ln1_g[l], ln1_b[l], w_up[l], w_down[l], ln2_g[l], ln2_b[l])
    return x
```

```python
import functools
import math

import numpy as np
import jax
import jax.numpy as jnp
from jax import lax
from jax.experimental import pallas as pl
from jax.experimental.pallas import tpu as pltpu

HEAD_DIM = 128
A_GROUPS = ((128, 1), (512, 4), (2048, 16))
A_HEADS_PER_GROUP = 4
A_HEADS = A_HEADS_PER_GROUP * len(A_GROUPS)
A_BLOCK = 64
A_GROUP_W = A_HEADS_PER_GROUP * HEAD_DIM
A_QKV_W = A_HEADS * HEAD_DIM
B_Q_HEADS = 8
B_KV_HEADS = 2
B_REP = B_Q_HEADS // B_KV_HEADS
B_HALF = 128
B_BLOCK = 128
B_Q_W = B_Q_HEADS * HEAD_DIM
B_KV_W = B_KV_HEADS * HEAD_DIM
NUM_BUCKETS = 32
MAX_DISTANCE = 1024
LN_EPS = 1e-5
NEG_INF = -1e30
QK_SCALE = HEAD_DIM ** -0.5

BF16 = jnp.bfloat16
F32 = jnp.float32

NAT_GA, NAT_GB, NAT_QB, NAT_Q0, NAT_K0, NAT_V0, NAT_KB, NAT_VB = 0, 2048, 4096, 5120, 5632, 6144, 6656, 6912
NAT_W = 7168

VMEM_LIMIT = 56 * 1024 * 1024


def _t5_bucket(rel):
    half = NUM_BUCKETS // 2
    max_exact = half // 2
    n = np.abs(rel)
    scaled = np.log(np.maximum(n, 1) / max_exact) / math.log(MAX_DISTANCE / max_exact)
    large = np.minimum(max_exact + (scaled * (half - max_exact)).astype(np.int64), half - 1)
    return (np.where(rel > 0, half, 0) + np.where(n < max_exact, n, large)).astype(np.int32)


def _band_offsets(blk):
    qi = np.arange(blk)[:, None]
    kj = np.arange(3 * blk)[None, :]
    return kj - blk - qi


def _band_masks(blk, half):
    rel = _band_offsets(blk)
    band = np.abs(rel) <= half
    kj = np.broadcast_to(np.arange(3 * blk)[None, :], band.shape)
    return np.stack([band, band & (kj >= blk), band & (kj < 2 * blk)])


def _layer_norm(y, g, b):
    mu = jnp.mean(y, axis=-1, keepdims=True)
    yc = y - mu
    var = jnp.mean(yc * yc, axis=-1, keepdims=True)
    return yc * lax.rsqrt(var + LN_EPS) * g + b


def _params(sem):
    return pltpu.CompilerParams(dimension_semantics=sem, vmem_limit_bytes=VMEM_LIMIT)


def _proj_nat_kernel(x_ref, w_ref, s_ref, o_ref, xb_ref):
    @pl.when(pl.program_id(1) == 0)
    def _():
        xb_ref[...] = x_ref[...].astype(BF16)

    acc = jnp.dot(xb_ref[...], w_ref[...], preferred_element_type=F32)
    o_ref[...] = (acc * s_ref[...]).astype(o_ref.dtype)


def _proj_nat(x, w, colscale, tm=1024, tn=1024):
    m, k = x.shape
    n = w.shape[1]
    return pl.pallas_call(
        _proj_nat_kernel,
        grid=(m // tm, n // tn),
        in_specs=[
            pl.BlockSpec((tm, k), lambda i, j: (i, 0)),
            pl.BlockSpec((k, tn), lambda i, j: (0, j)),
            pl.BlockSpec((1, tn), lambda i, j: (0, j)),
        ],
        out_specs=pl.BlockSpec((tm, tn), lambda i, j: (i, j)),
        out_shape=jax.ShapeDtypeStruct((m, n), BF16),
        scratch_shapes=[pltpu.VMEM((tm, k), BF16)],
        compiler_params=_params(("arbitrary", "arbitrary")),
        name="proj_nat",
    )(x, w, colscale)


def _proj_perm_kernel(x_ref, w4_ref, w16_ref, s_ref, o4_ref, o16_ref, xb_ref, res_ref, *, tm):
    xb_ref[...] = x_ref[...].astype(BF16)
    ncb = res_ref.shape[0]
    for d, w_ref, o_ref in ((4, w4_ref, o4_ref), (16, w16_ref, o16_ref)):
        n = tm // d
        res = jnp.dot(xb_ref[...], w_ref[...], preferred_element_type=F32) * s_ref[...]
        for cb in range(ncb):
            res_ref[cb] = res[:, cb * HEAD_DIM:(cb + 1) * HEAD_DIM]
        for r in range(d):
            for cb in range(ncb):
                o_ref[0, r, :, cb * HEAD_DIM:(cb + 1) * HEAD_DIM] = (
                    res_ref[cb, pl.ds(r, n, stride=d), :].astype(BF16))


def _proj_perm(x, w4, w16, colscale, batch, seq, tm=1024):
    m, k = x.shape
    n = w4.shape[1]
    nt = seq // tm
    outs = []
    out_specs = []
    for d in (4, 16):
        outs.append(jax.ShapeDtypeStruct((batch, d, seq // d, n), BF16))
        out_specs.append(pl.BlockSpec((1, d, tm // d, n), lambda i: (i // nt, 0, i % nt, 0)))
    return pl.pallas_call(
        functools.partial(_proj_perm_kernel, tm=tm),
        grid=(m // tm,),
        in_specs=[
            pl.BlockSpec((tm, k), lambda i: (i, 0)),
            pl.BlockSpec((k, n), lambda i: (0, 0)),
            pl.BlockSpec((k, n), lambda i: (0, 0)),
            pl.BlockSpec((1, n), lambda i: (0, 0)),
        ],
        out_specs=out_specs,
        out_shape=outs,
        scratch_shapes=[pltpu.VMEM((tm, k), BF16), pltpu.VMEM((n // HEAD_DIM, tm, HEAD_DIM), F32)],
        compiler_params=_params(("arbitrary",)),
        name="proj_perm",
    )(x, w4, w16, colscale)


def _band_block(q, kw, vw, bias, sink=None):
    s = lax.dot_general(q, kw, (((1,), (1,)), ((), ())), preferred_element_type=F32) + bias
    m = jnp.max(s, axis=-1, keepdims=True)
    if sink is not None:
        m = jnp.maximum(m, sink)
    p = jnp.exp(s - m)
    den = jnp.sum(p, axis=-1, keepdims=True)
    if sink is not None:
        den = den + jnp.exp(sink - m)
    acc = jnp.dot(p.astype(BF16), vw, preferred_element_type=F32)
    return acc, m, den


def _edge_variant(c, nblk, t, nt):
    first = jnp.logical_and(c == 0, t == 0)
    last = jnp.logical_and(c == nblk - 1, t == nt - 1)
    return jnp.where(first, 1, jnp.where(last, 2, 0))


def _attn_a_kernel(*refs, tq):
    ins = refs[:21]
    bias_ref = refs[21]
    o_ref = refs[22]
    og_ref, lse_ref = refs[23:25]
    win_refs = refs[25:29]
    t = pl.program_id(1)
    nt = pl.num_programs(1)
    blk = A_BLOCK

    for g, (_, d) in enumerate(A_GROUPS):
        q_ref, kp_ref, k_ref, kn_ref, vp_ref, v_ref, vn_ref = ins[7 * g:7 * g + 7]
        n = tq // d
        nblk = n // blk

        if nblk > 1:
            kw_ref, vw_ref = win_refs[2 * g:2 * g + 2]
            for w_ref, p_ref, c_ref, n_ref in ((kw_ref, kp_ref, k_ref, kn_ref), (vw_ref, vp_ref, v_ref, vn_ref)):
                w_ref[:, 0:blk, :] = p_ref[...].reshape(d, blk, A_GROUP_W)
                w_ref[:, blk:blk + n, :] = c_ref[...].reshape(d, n, A_GROUP_W)
                w_ref[:, blk + n:, :] = n_ref[...].reshape(d, blk, A_GROUP_W)

        def block(it, carry, g=g, d=d, nblk=nblk):
            r = it // nblk
            c = it % nblk
            variant = _edge_variant(c, nblk, t, nt)
            qrows = pl.ds(pl.multiple_of(c * blk, blk), blk)
            wrows = pl.ds(pl.multiple_of(c * blk, blk), 3 * blk)
            if g == 0:
                q = q_ref[qrows, :]
            else:
                q = q_ref[0, r, qrows, :]
            if nblk > 1:
                kw = kw_ref[r, wrows, :]
                vw = vw_ref[r, wrows, :]
            else:
                kw = jnp.concatenate([kp_ref[0, r], k_ref[0, r], kn_ref[0, r]], axis=0)
                vw = jnp.concatenate([vp_ref[0, r], v_ref[0, r], vn_ref[0, r]], axis=0)
            if d == 1:
                dst = qrows
            else:
                dst = pl.ds(r + d * blk * c, blk, stride=d)
            for h in range(A_HEADS_PER_GROUP):
                cols = slice(h * HEAD_DIM, (h + 1) * HEAD_DIM)
                acc, m, den = _band_block(q[:, cols], kw[:, cols], vw[:, cols], bias_ref[g, variant, h])
                og_ref[g, h, dst, :] = acc / den
                lse_ref[g, h, dst, :] = jnp.broadcast_to(m + jnp.log(den), acc.shape)
            return carry

        lax.fori_loop(0, tq // blk, block, 0)

    chunk = 128

    def merge(i, carry):
        rs = pl.ds(pl.multiple_of(i * chunk, chunk), chunk)
        for h in range(A_HEADS_PER_GROUP):
            ls = [lse_ref[g, h, rs, :] for g in range(3)]
            mx = jnp.maximum(jnp.maximum(ls[0], ls[1]), ls[2])
            num = jnp.zeros((chunk, HEAD_DIM), F32)
            dn = jnp.zeros((chunk, HEAD_DIM), F32)
            for g in range(3):
                e = jnp.exp(ls[g] - mx)
                num = num + e * og_ref[g, h, rs, :]
                dn = dn + e
            o_ref[rs, h * HEAD_DIM:(h + 1) * HEAD_DIM] = (num / dn).astype(o_ref.dtype)
        return carry

    lax.fori_loop(0, tq // chunk, merge, 0)


def _attn_a(p_nat, a4, a16, bias, batch, seq, tq=1024):
    m = p_nat.shape[0]
    nt = seq // tq
    blk = A_BLOCK
    gw = A_GROUP_W
    in_specs = []
    args = []

    nb_tile = tq // blk
    last64 = m // blk - 1
    q_col, k_col, v_col = NAT_Q0 // gw, NAT_K0 // gw, NAT_V0 // gw

    def g0_main(col):
        return pl.BlockSpec((tq, gw), lambda b, t: (b * nt + t, col))

    def g0_prev(col):
        return pl.BlockSpec((blk, gw), lambda b, t: (jnp.maximum((b * nt + t) * nb_tile - 1, 0), col))

    def g0_next(col):
        return pl.BlockSpec((blk, gw), lambda b, t: (jnp.minimum((b * nt + t + 1) * nb_tile, last64), col))

    in_specs += [g0_main(q_col), g0_prev(k_col), g0_main(k_col), g0_next(k_col),
                 g0_prev(v_col), g0_main(v_col), g0_next(v_col)]
    args += [p_nat] * 7

    for arr, d in ((a4, 4), (a16, 16)):
        n = tq // d
        nb = n // blk
        lastb = seq // d // blk - 1

        def main(col, d=d, n=n):
            return pl.BlockSpec((1, d, n, gw), lambda b, t: (b, 0, t, col))

        def prev(col, d=d, nb=nb):
            return pl.BlockSpec((1, d, blk, gw), lambda b, t: (b, 0, jnp.maximum(t * nb - 1, 0), col))

        def nxt(col, d=d, nb=nb, lastb=lastb):
            return pl.BlockSpec((1, d, blk, gw), lambda b, t: (b, 0, jnp.minimum((t + 1) * nb, lastb), col))

        in_specs += [main(0), prev(1), main(1), nxt(1), prev(2), main(2), nxt(2)]
        args += [arr] * 7

    in_specs.append(pl.BlockSpec(bias.shape, lambda b, t: (0,) * bias.ndim))
    args.append(bias)

    return pl.pallas_call(
        functools.partial(_attn_a_kernel, tq=tq),
        grid=(batch, nt),
        in_specs=in_specs,
        out_specs=pl.BlockSpec((tq, gw), lambda b, t: (b * nt + t, 0)),
        out_shape=jax.ShapeDtypeStruct((m, gw), BF16),
        scratch_shapes=(
            [pltpu.VMEM((3, A_HEADS_PER_GROUP, tq, HEAD_DIM), F32)] * 2
            + [pltpu.VMEM((d, tq // d + 2 * blk, gw), BF16) for _, d in A_GROUPS[:2] for _ in range(2)]),
        compiler_params=_params(("arbitrary", "arbitrary")),
        name="attn_a",
    )(*args)


def _attn_b_kernel(q_ref, kp_ref, k_ref, kn_ref, vp_ref, v_ref, vn_ref, bias_ref, sink_ref, o_ref,
                   kw_ref, vw_ref, *, tq):
    t = pl.program_id(1)
    nt = pl.num_programs(1)
    blk = B_BLOCK
    nblk = tq // blk

    for w_ref, p_ref, c_ref, n_ref in ((kw_ref, kp_ref, k_ref, kn_ref), (vw_ref, vp_ref, v_ref, vn_ref)):
        w_ref[0:blk, :] = p_ref[...]
        w_ref[blk:blk + tq, :] = c_ref[...]
        w_ref[blk + tq:, :] = n_ref[...]

    def block(c, carry):
        variant = _edge_variant(c, nblk, t, nt)
        rows = pl.ds(pl.multiple_of(c * blk, blk), blk)
        wrows = pl.ds(pl.multiple_of(c * blk, blk), 3 * blk)
        for kvh in range(B_KV_HEADS):
            kcols = slice(kvh * HEAD_DIM, (kvh + 1) * HEAD_DIM)
            q4 = jnp.concatenate(
                [q_ref[rows, (kvh * B_REP + rep) * HEAD_DIM:(kvh * B_REP + rep + 1) * HEAD_DIM]
                 for rep in range(B_REP)], axis=0)
            acc, _, den = _band_block(q4, kw_ref[wrows, kcols], vw_ref[wrows, kcols],
                                      bias_ref[variant, kvh], sink_ref[kvh])
            o = (acc / den).astype(o_ref.dtype)
            for rep in range(B_REP):
                h = kvh * B_REP + rep
                o_ref[rows, h * HEAD_DIM:(h + 1) * HEAD_DIM] = o[rep * blk:(rep + 1) * blk, :]
        return carry

    lax.fori_loop(0, nblk, block, 0)


def _attn_b(p_nat, bias, sink, batch, seq, tq=1024):
    m = p_nat.shape[0]
    nt = seq // tq
    blk = B_BLOCK
    nb_tile = tq // blk
    lastb = m // blk - 1
    q_col = NAT_QB // B_Q_W
    k_col = NAT_KB // B_KV_W
    v_col = NAT_VB // B_KV_W

    def main(col):
        return pl.BlockSpec((tq, B_KV_W), lambda b, t: (b * nt + t, col))

    def prev(col):
        return pl.BlockSpec((blk, B_KV_W), lambda b, t: (jnp.maximum((b * nt + t) * nb_tile - 1, 0), col))

    def nxt(col):
        return pl.BlockSpec((blk, B_KV_W), lambda b, t: (jnp.minimum((b * nt + t + 1) * nb_tile, lastb), col))

    return pl.pallas_call(
        functools.partial(_attn_b_kernel, tq=tq),
        grid=(batch, nt),
        in_specs=[
            pl.BlockSpec((tq, B_Q_W), lambda b, t: (b * nt + t, q_col)),
            prev(k_col), main(k_col), nxt(k_col),
            prev(v_col), main(v_col), nxt(v_col),
            pl.BlockSpec(bias.shape, lambda b, t: (0, 0, 0, 0)),
            pl.BlockSpec(sink.shape, lambda b, t: (0, 0, 0)),
        ],
        out_specs=pl.BlockSpec((tq, B_Q_W), lambda b, t: (b * nt + t, 0)),
        out_shape=jax.ShapeDtypeStruct((m, B_Q_W), BF16),
        scratch_shapes=[pltpu.VMEM((tq + 2 * blk, B_KV_W), BF16)] * 2,
        compiler_params=_params(("arbitrary", "arbitrary")),
        name="attn_b",
    )(p_nat, p_nat, p_nat, p_nat, p_nat, p_nat, p_nat, bias, sink)


def _mix_ln_kernel(x_ref, oa_ref, ob_ref, ga_ref, gb_ref, wpa_ref, wpb_ref, wo_ref, g_ref, b_ref, o_ref, *, alpha):
    ya = jnp.dot(oa_ref[...], wpa_ref[...], preferred_element_type=F32)
    yb = jnp.dot(ob_ref[...], wpb_ref[...], preferred_element_type=F32)
    merged = jax.nn.sigmoid(ga_ref[...].astype(F32)) * ya + jax.nn.sigmoid(gb_ref[...].astype(F32)) * yb
    z = jnp.dot(merged.astype(BF16), wo_ref[...], preferred_element_type=F32)
    o_ref[...] = _layer_norm(alpha * x_ref[...] + z, g_ref[...], b_ref[...])


def _mix_ln(x, oa, ob, p_nat, w_pa, w_pb, w_out, ln_g, ln_b, alpha, tm=512):
    m, dm = x.shape

    def const(arr):
        return pl.BlockSpec(arr.shape, lambda i: (0, 0), pipeline_mode=pl.Buffered(1))

    return pl.pallas_call(
        functools.partial(_mix_ln_kernel, alpha=alpha),
        grid=(m // tm,),
        in_specs=[
            pl.BlockSpec((tm, dm), lambda i: (i, 0)),
            pl.BlockSpec((tm, oa.shape[1]), lambda i: (i, 0)),
            pl.BlockSpec((tm, ob.shape[1]), lambda i: (i, 0)),
            pl.BlockSpec((tm, dm), lambda i: (i, NAT_GA // dm)),
            pl.BlockSpec((tm, dm), lambda i: (i, NAT_GB // dm)),
            const(w_pa), const(w_pb), const(w_out), const(ln_g), const(ln_b),
        ],
        out_specs=pl.BlockSpec((tm, dm), lambda i: (i, 0)),
        out_shape=jax.ShapeDtypeStruct((m, dm), F32),
        compiler_params=_params(("arbitrary",)),
        name="mix_ln",
    )(x, oa, ob, p_nat, p_nat, w_pa, w_pb, w_out, ln_g, ln_b)


def _ffn_ln_kernel(x_ref, wu_ref, wd_ref, g_ref, b_ref, o_ref, xb_ref, *, alpha, tf):
    f = pl.program_id(1)
    nf = pl.num_programs(1)

    @pl.when(f == 0)
    def _():
        xb_ref[...] = x_ref[...].astype(BF16)

    h = jnp.dot(xb_ref[...], wu_ref[...], preferred_element_type=F32)
    gate = h[:, :tf]
    up = h[:, tf:]
    act = (gate * jax.nn.sigmoid(gate) * up).astype(BF16)
    part = jnp.dot(act, wd_ref[...], preferred_element_type=F32)

    @pl.when(f == 0)
    def _():
        o_ref[...] = alpha * x_ref[...] + part

    @pl.when(f > 0)
    def _():
        o_ref[...] += part

    @pl.when(f == nf - 1)
    def _():
        o_ref[...] = _layer_norm(o_ref[...], g_ref[...], b_ref[...])


def _ffn_ln(x, w_up, w_down, ln_g, ln_b, alpha, tf, tm=512):
    m, dm = x.shape
    nf = w_down.shape[0] // tf
    return pl.pallas_call(
        functools.partial(_ffn_ln_kernel, alpha=alpha, tf=tf),
        grid=(m // tm, nf),
        in_specs=[
            pl.BlockSpec((tm, dm), lambda i, f: (i, 0)),
            pl.BlockSpec((dm, 2 * tf), lambda i, f: (0, f)),
            pl.BlockSpec((tf, dm), lambda i, f: (f, 0)),
            pl.BlockSpec((1, dm), lambda i, f: (0, 0)),
            pl.BlockSpec((1, dm), lambda i, f: (0, 0)),
        ],
        out_specs=pl.BlockSpec((tm, dm), lambda i, f: (i, 0)),
        out_shape=jax.ShapeDtypeStruct((m, dm), F32),
        scratch_shapes=[pltpu.VMEM((tm, dm), BF16)],
        compiler_params=_params(("arbitrary", "arbitrary")),
        name="ffn_ln",
    )(x, w_up, w_down, ln_g, ln_b)


def _bias_tables(rel_bias):
    tabs = []
    for g, (window, dil) in enumerate(A_GROUPS):
        idx = _t5_bucket(_band_offsets(A_BLOCK) * dil)
        heads = rel_bias[idx][..., g * A_HEADS_PER_GROUP:(g + 1) * A_HEADS_PER_GROUP]
        heads = heads.transpose(2, 0, 1).astype(F32)
        masks = _band_masks(A_BLOCK, window // (2 * dil))
        tabs.append(jnp.where(masks[:, None], heads[None], NEG_INF))
    bias_a = jnp.stack(tabs)

    idx = _t5_bucket(_band_offsets(B_BLOCK))
    heads = rel_bias[idx][..., A_HEADS:].transpose(2, 0, 1).astype(F32)
    heads = heads.reshape(B_KV_HEADS, B_REP * B_BLOCK, 3 * B_BLOCK)
    masks = np.tile(_band_masks(B_BLOCK, B_HALF), (1, B_REP, 1))
    bias_b = jnp.where(masks[:, None], heads[None], NEG_INF)
    return bias_a, bias_b


def _prep_w_in(w_in):
    qa, ka, va = (w_in[:, i * A_QKV_W:(i + 1) * A_QKV_W] for i in range(3))
    o = 3 * A_QKV_W
    qb = w_in[:, o:o + B_Q_W]
    kb = w_in[:, o + B_Q_W:o + B_Q_W + B_KV_W]
    vb = w_in[:, o + B_Q_W + B_KV_W:o + B_Q_W + 2 * B_KV_W]
    o += B_Q_W + 2 * B_KV_W
    dm = w_in.shape[0]
    ga = w_in[:, o:o + dm]
    gb = w_in[:, o + dm:o + 2 * dm]

    def grp(w, g):
        return w[:, g * A_GROUP_W:(g + 1) * A_GROUP_W]

    w_nat = jnp.concatenate([ga, gb, qb, grp(qa, 0), grp(ka, 0), grp(va, 0), kb, vb], axis=1).astype(BF16)
    w_perm = [jnp.concatenate([grp(qa, g), grp(ka, g), grp(va, g)], axis=1).astype(BF16) for g in (1, 2)]
    s_nat = np.ones((1, NAT_W), np.float32)
    s_nat[:, NAT_QB:NAT_QB + B_Q_W] = QK_SCALE
    s_nat[:, NAT_Q0:NAT_Q0 + A_GROUP_W] = QK_SCALE
    s_perm = np.ones((1, 3 * A_GROUP_W), np.float32)
    s_perm[:, :A_GROUP_W] = QK_SCALE
    return w_nat, w_perm, jnp.asarray(s_nat), jnp.asarray(s_perm)


def _prep_w_up(w_up, tf):
    dm, two_ff = w_up.shape
    ff = two_ff // 2
    nf = ff // tf
    gate = w_up[:, :ff].reshape(dm, nf, tf)
    up = w_up[:, ff:].reshape(dm, nf, tf)
    return jnp.concatenate([gate, up], axis=2).reshape(dm, two_ff).astype(BF16)


def kernel(x, rel_bias, w_in, sink, w_pa, w_pb, w_out, ln1_g, ln1_b, w_up, w_down, ln2_g, ln2_b):
    batch, seq, dm = x.shape
    depth = w_in.shape[0]
    alpha = (2 * depth) ** 0.25
    tf = 512
    bias_a, bias_b = _bias_tables(rel_bias)
    h = x.reshape(batch * seq, dm)
    for l in range(depth):
        w_nat, (w4, w16), s_nat, s_perm = _prep_w_in(w_in[l])
        sink_col = jnp.repeat(sink[l].reshape(B_KV_HEADS, B_REP), B_BLOCK, axis=1)[..., None].astype(F32)
        p_nat = _proj_nat(h, w_nat, s_nat)
        a4, a16 = _proj_perm(h, w4, w16, s_perm, batch, seq)
        oa = _attn_a(p_nat, a4, a16, bias_a, batch, seq)
        ob = _attn_b(p_nat, bias_b, sink_col, batch, seq)
        h = _mix_ln(h, oa, ob, p_nat, w_pa[l].astype(BF16), w_pb[l].astype(BF16), w_out[l].astype(BF16),
                    ln1_g[l][None], ln1_b[l][None], alpha)
        h = _ffn_ln(h, _prep_w_up(w_up[l], tf), w_down[l].astype(BF16), ln2_g[l][None], ln2_b[l][None], alpha, tf)
    return h.reshape(batch, seq, dm)
```

```python
import functools
import math

import numpy as np
import jax
import jax.numpy as jnp
from jax import lax
from jax.experimental import pallas as pl
from jax.experimental.pallas import tpu as pltpu

HEAD_DIM = 128
A_GROUPS = ((128, 1), (512, 4), (2048, 16))
A_HEADS_PER_GROUP = 4
A_HEADS = A_HEADS_PER_GROUP * len(A_GROUPS)
A_BLOCK = 64
A_GROUP_W = A_HEADS_PER_GROUP * HEAD_DIM
A_QKV_W = A_HEADS * HEAD_DIM
B_Q_HEADS = 8
B_KV_HEADS = 2
B_REP = B_Q_HEADS // B_KV_HEADS
B_HALF = 128
B_BLOCK = 128
B_Q_W = B_Q_HEADS * HEAD_DIM
B_KV_W = B_KV_HEADS * HEAD_DIM
NUM_BUCKETS = 32
MAX_DISTANCE = 1024
LN_EPS = 1e-5
NEG_INF = -1e30
QK_SCALE = HEAD_DIM ** -0.5
A_UNROLL = 4
B_ONES_ROWS = 16
FFN_SUBTILES = 2

BF16 = jnp.bfloat16
F32 = jnp.float32

NAT_GA, NAT_GB, NAT_QB, NAT_Q0, NAT_K0, NAT_V0, NAT_KB, NAT_VB = 0, 2048, 4096, 5120, 5632, 6144, 6656, 6912
NAT_W = 7168

VMEM_LIMIT = 56 * 1024 * 1024


def _t5_bucket(rel):
    half = NUM_BUCKETS // 2
    max_exact = half // 2
    n = np.abs(rel)
    scaled = np.log(np.maximum(n, 1) / max_exact) / math.log(MAX_DISTANCE / max_exact)
    large = np.minimum(max_exact + (scaled * (half - max_exact)).astype(np.int64), half - 1)
    return (np.where(rel > 0, half, 0) + np.where(n < max_exact, n, large)).astype(np.int32)


def _band_offsets(blk):
    qi = np.arange(blk)[:, None]
    kj = np.arange(3 * blk)[None, :]
    return kj - blk - qi


def _band_masks(blk, half):
    rel = _band_offsets(blk)
    band = np.abs(rel) <= half
    kj = np.broadcast_to(np.arange(3 * blk)[None, :], band.shape)
    return np.stack([band, band & (kj >= blk), band & (kj < 2 * blk)])


def _layer_norm(y, g, b):
    mu = jnp.mean(y, axis=-1, keepdims=True)
    yc = y - mu
    var = jnp.mean(yc * yc, axis=-1, keepdims=True)
    return yc * lax.rsqrt(var + LN_EPS) * g + b


def _params(sem):
    return pltpu.CompilerParams(dimension_semantics=sem, vmem_limit_bytes=VMEM_LIMIT)


def _proj_nat_kernel(x_ref, w_ref, s_ref, o_ref, vt_ref, xb_ref):
    j = pl.program_id(1)

    @pl.when(j == 0)
    def _():
        xb_ref[...] = x_ref[...].astype(BF16)

    res = jnp.dot(xb_ref[...], w_ref[...], preferred_element_type=F32) * s_ref[...]
    o_ref[...] = res.astype(o_ref.dtype)

    @pl.when(j == pl.num_programs(1) - 1)
    def _():
        vt_ref[...] = res[:, res.shape[1] - B_KV_W:].T.astype(vt_ref.dtype)


def _proj_nat(x, w, colscale, tm=1024, tn=1024):
    m, k = x.shape
    n = w.shape[1]
    return pl.pallas_call(
        _proj_nat_kernel,
        grid=(m // tm, n // tn),
        in_specs=[
            pl.BlockSpec((tm, k), lambda i, j: (i, 0)),
            pl.BlockSpec((k, tn), lambda i, j: (0, j)),
            pl.BlockSpec((1, tn), lambda i, j: (0, j)),
        ],
        out_specs=[pl.BlockSpec((tm, tn), lambda i, j: (i, j)),
                   pl.BlockSpec((B_KV_W, tm), lambda i, j: (0, i))],
        out_shape=[jax.ShapeDtypeStruct((m, n), BF16), jax.ShapeDtypeStruct((B_KV_W, m), BF16)],
        scratch_shapes=[pltpu.VMEM((tm, k), BF16)],
        compiler_params=_params(("arbitrary", "arbitrary")),
        name="proj_nat",
    )(x, w, colscale)


def _proj_perm_kernel(x_ref, w4_ref, w16_ref, s_ref, o4_ref, o16_ref, xb_ref, res_ref, *, tm):
    xb_ref[...] = x_ref[...].astype(BF16)
    ncb = res_ref.shape[0]
    for d, w_ref, o_ref in ((4, w4_ref, o4_ref), (16, w16_ref, o16_ref)):
        n = tm // d
        res = jnp.dot(xb_ref[...], w_ref[...], preferred_element_type=F32) * s_ref[...]
        for cb in range(ncb):
            res_ref[cb] = res[:, cb * HEAD_DIM:(cb + 1) * HEAD_DIM]
        for r in range(d):
            for cb in range(ncb):
                o_ref[0, r, :, cb * HEAD_DIM:(cb + 1) * HEAD_DIM] = (
                    res_ref[cb, pl.ds(r, n, stride=d), :].astype(BF16))


def _proj_perm(x, w4, w16, colscale, batch, seq, tm=1024):
    m, k = x.shape
    n = w4.shape[1]
    nt = seq // tm
    outs = []
    out_specs = []
    for d in (4, 16):
        outs.append(jax.ShapeDtypeStruct((batch, d, seq // d, n), BF16))
        out_specs.append(pl.BlockSpec((1, d, tm // d, n), lambda i: (i // nt, 0, i % nt, 0)))
    return pl.pallas_call(
        functools.partial(_proj_perm_kernel, tm=tm),
        grid=(m // tm,),
        in_specs=[
            pl.BlockSpec((tm, k), lambda i: (i, 0)),
            pl.BlockSpec((k, n), lambda i: (0, 0)),
            pl.BlockSpec((k, n), lambda i: (0, 0)),
            pl.BlockSpec((1, n), lambda i: (0, 0)),
        ],
        out_specs=out_specs,
        out_shape=outs,
        scratch_shapes=[pltpu.VMEM((tm, k), BF16), pltpu.VMEM((n // HEAD_DIM, tm, HEAD_DIM), F32)],
        compiler_params=_params(("arbitrary",)),
        name="proj_perm",
    )(x, w4, w16, colscale)


def _band_block(q, kw, vw, bias, sink=None):
    s = lax.dot_general(q, kw, (((1,), (1,)), ((), ())), preferred_element_type=F32) + bias
    m = jnp.max(s, axis=-1, keepdims=True)
    if sink is not None:
        m = jnp.maximum(m, sink)
    p = jnp.exp(s - m)
    den = jnp.sum(p, axis=-1, keepdims=True)
    if sink is not None:
        den = den + jnp.exp(sink - m)
    acc = jnp.dot(p.astype(BF16), vw, preferred_element_type=F32)
    return acc, m, den


def _edge_variant(c, nblk, t, nt):
    first = jnp.logical_and(c == 0, t == 0)
    last = jnp.logical_and(c == nblk - 1, t == nt - 1)
    return jnp.where(first, 1, jnp.where(last, 2, 0))


def _attn_a_kernel(*refs, tq):
    ins = refs[:21]
    bias_ref = refs[21]
    o_ref = refs[22]
    og_ref, lse_ref = refs[23:25]
    win_refs = refs[25:29]
    t = pl.program_id(1)
    nt = pl.num_programs(1)
    blk = A_BLOCK

    for g, (_, d) in enumerate(A_GROUPS):
        q_ref, kp_ref, k_ref, kn_ref, vp_ref, v_ref, vn_ref = ins[7 * g:7 * g + 7]
        n = tq // d
        nblk = n // blk

        if nblk > 1:
            kw_ref, vw_ref = win_refs[2 * g:2 * g + 2]
            for w_ref, p_ref, c_ref, n_ref in ((kw_ref, kp_ref, k_ref, kn_ref), (vw_ref, vp_ref, v_ref, vn_ref)):
                w_ref[:, 0:blk, :] = p_ref[...].reshape(d, blk, A_GROUP_W)
                w_ref[:, blk:blk + n, :] = c_ref[...].reshape(d, n, A_GROUP_W)
                w_ref[:, blk + n:, :] = n_ref[...].reshape(d, blk, A_GROUP_W)

        def block(it, carry, g=g, d=d, nblk=nblk):
            qs, kws, vws, biases, dsts = [], [], [], [], []
            for j in range(A_UNROLL):
                idx = it * A_UNROLL + j
                r = idx // nblk
                c = idx % nblk
                variant = _edge_variant(c, nblk, t, nt)
                qrows = pl.ds(pl.multiple_of(c * blk, blk), blk)
                wrows = pl.ds(pl.multiple_of(c * blk, blk), 3 * blk)
                if g == 0:
                    q = q_ref[qrows, :]
                else:
                    q = q_ref[0, r, qrows, :]
                if nblk > 1:
                    kw = kw_ref[r, wrows, :]
                    vw = vw_ref[r, wrows, :]
                else:
                    kw = jnp.concatenate([kp_ref[0, r], k_ref[0, r], kn_ref[0, r]], axis=0)
                    vw = jnp.concatenate([vp_ref[0, r], v_ref[0, r], vn_ref[0, r]], axis=0)
                if d == 1:
                    dst = qrows
                else:
                    dst = pl.ds(r + d * blk * c, blk, stride=d)
                for h in range(A_HEADS_PER_GROUP):
                    cols = slice(h * HEAD_DIM, (h + 1) * HEAD_DIM)
                    qs.append(q[:, cols])
                    kws.append(kw[:, cols])
                    vws.append(vw[:, cols])
                    biases.append(bias_ref[g, variant, h])
                    dsts.append((h, dst))
            q3 = jnp.stack(qs)
            kw3 = jnp.stack(kws)
            vw3 = jnp.stack(vws)
            s = jnp.einsum("uqd,ukd->uqk", q3, kw3, preferred_element_type=F32) + jnp.stack(biases)
            m = jnp.max(s, axis=-1, keepdims=True)
            p = jnp.exp(s - m).astype(BF16)
            vx = jnp.concatenate([vw3, jnp.ones_like(vw3)], axis=-1)
            acc = jnp.einsum("uqk,ukd->uqd", p, vx, preferred_element_type=F32)
            num = acc[..., :HEAD_DIM]
            den = acc[..., HEAD_DIM:]
            og = num / den
            lse = m + jnp.log(den)
            for u, (h, dst) in enumerate(dsts):
                og_ref[g, h, dst, :] = og[u]
                lse_ref[g, h, dst, :] = lse[u]
            return carry

        lax.fori_loop(0, tq // blk // A_UNROLL, block, 0)

    chunk = 128

    def merge(i, carry):
        rs = pl.ds(pl.multiple_of(i * chunk, chunk), chunk)
        for h in range(A_HEADS_PER_GROUP):
            ls = [lse_ref[g, h, rs, :] for g in range(3)]
            mx = jnp.maximum(jnp.maximum(ls[0], ls[1]), ls[2])
            num = jnp.zeros((chunk, HEAD_DIM), F32)
            dn = jnp.zeros((chunk, HEAD_DIM), F32)
            for g in range(3):
                e = jnp.exp(ls[g] - mx)
                num = num + e * og_ref[g, h, rs, :]
                dn = dn + e
            o_ref[rs, h * HEAD_DIM:(h + 1) * HEAD_DIM] = (num / dn).astype(o_ref.dtype)
        return carry

    lax.fori_loop(0, tq // chunk, merge, 0)


def _attn_a(p_nat, a4, a16, bias, batch, seq, tq=1024):
    m = p_nat.shape[0]
    nt = seq // tq
    blk = A_BLOCK
    gw = A_GROUP_W
    in_specs = []
    args = []

    nb_tile = tq // blk
    last64 = m // blk - 1
    q_col, k_col, v_col = NAT_Q0 // gw, NAT_K0 // gw, NAT_V0 // gw

    def g0_main(col):
        return pl.BlockSpec((tq, gw), lambda b, t: (b * nt + t, col))

    def g0_prev(col):
        return pl.BlockSpec((blk, gw), lambda b, t: (jnp.maximum((b * nt + t) * nb_tile - 1, 0), col))

    def g0_next(col):
        return pl.BlockSpec((blk, gw), lambda b, t: (jnp.minimum((b * nt + t + 1) * nb_tile, last64), col))

    in_specs += [g0_main(q_col), g0_prev(k_col), g0_main(k_col), g0_next(k_col),
                 g0_prev(v_col), g0_main(v_col), g0_next(v_col)]
    args += [p_nat] * 7

    for arr, d in ((a4, 4), (a16, 16)):
        n = tq // d
        nb = n // blk
        lastb = seq // d // blk - 1

        def main(col, d=d, n=n):
            return pl.BlockSpec((1, d, n, gw), lambda b, t: (b, 0, t, col))

        def prev(col, d=d, nb=nb):
            return pl.BlockSpec((1, d, blk, gw), lambda b, t: (b, 0, jnp.maximum(t * nb - 1, 0), col))

        def nxt(col, d=d, nb=nb, lastb=lastb):
            return pl.BlockSpec((1, d, blk, gw), lambda b, t: (b, 0, jnp.minimum((t + 1) * nb, lastb), col))

        in_specs += [main(0), prev(1), main(1), nxt(1), prev(2), main(2), nxt(2)]
        args += [arr] * 7

    in_specs.append(pl.BlockSpec(bias.shape, lambda b, t: (0,) * bias.ndim))
    args.append(bias)

    return pl.pallas_call(
        functools.partial(_attn_a_kernel, tq=tq),
        grid=(batch, nt),
        in_specs=in_specs,
        out_specs=pl.BlockSpec((tq, gw), lambda b, t: (b * nt + t, 0)),
        out_shape=jax.ShapeDtypeStruct((m, gw), BF16),
        scratch_shapes=(
            [pltpu.VMEM((3, A_HEADS_PER_GROUP, tq, HEAD_DIM), F32)] * 2
            + [pltpu.VMEM((d, tq // d + 2 * blk, gw), BF16) for _, d in A_GROUPS[:2] for _ in range(2)]),
        compiler_params=_params(("arbitrary", "arbitrary")),
        name="attn_a",
    )(*args)


def _attn_b_kernel(q_ref, kp_ref, k_ref, kn_ref, vp_ref, v_ref, vn_ref, bias_ref, sink_ref, o_ref,
                   kw_ref, vw_ref, *, tq):
    t = pl.program_id(1)
    nt = pl.num_programs(1)
    blk = B_BLOCK
    nblk = tq // blk

    kw_ref[0:blk, :] = kp_ref[...]
    kw_ref[blk:blk + tq, :] = k_ref[...]
    kw_ref[blk + tq:, :] = kn_ref[...]
    vw_ref[:, 0:blk] = vp_ref[...]
    vw_ref[:, blk:blk + tq] = v_ref[...]
    vw_ref[:, blk + tq:] = vn_ref[...]
    ones = jnp.ones((B_ONES_ROWS, 3 * blk), BF16)

    def block(c, carry):
        variant = _edge_variant(c, nblk, t, nt)
        rows = pl.ds(pl.multiple_of(c * blk, blk), blk)
        wrows = pl.ds(pl.multiple_of(c * blk, blk), 3 * blk)
        for kvh in range(B_KV_HEADS):
            kcols = slice(kvh * HEAD_DIM, (kvh + 1) * HEAD_DIM)
            q4 = jnp.concatenate(
                [q_ref[rows, (kvh * B_REP + rep) * HEAD_DIM:(kvh * B_REP + rep + 1) * HEAD_DIM]
                 for rep in range(B_REP)], axis=0)
            s = lax.dot_general(kw_ref[wrows, kcols], q4, (((1,), (1,)), ((), ())),
                                preferred_element_type=F32) + bias_ref[variant, kvh]
            sink = sink_ref[kvh]
            m = jnp.maximum(jnp.max(s, axis=0, keepdims=True), sink)
            p = jnp.exp(s - m).astype(BF16)
            vx = jnp.concatenate([vw_ref[kcols, wrows], ones], axis=0)
            acc = jnp.dot(vx, p, preferred_element_type=F32)
            den = acc[HEAD_DIM:HEAD_DIM + 1, :] + jnp.exp(sink - m)
            o = (acc[:HEAD_DIM, :] / den).astype(o_ref.dtype)
            for rep in range(B_REP):
                h = kvh * B_REP + rep
                o_ref[h * HEAD_DIM:(h + 1) * HEAD_DIM, rows] = o[:, rep * blk:(rep + 1) * blk]
        return carry

    lax.fori_loop(0, nblk, block, 0)


def _attn_b(p_nat, vbt, bias, sink, batch, seq, tq=1024):
    m = p_nat.shape[0]
    nt = seq // tq
    blk = B_BLOCK
    nb_tile = tq // blk
    lastb = m // blk - 1
    q_col = NAT_QB // B_Q_W
    k_col = NAT_KB // B_KV_W

    def first(b, t):
        return jnp.maximum((b * nt + t) * nb_tile - 1, 0)

    def after(b, t):
        return jnp.minimum((b * nt + t + 1) * nb_tile, lastb)

    return pl.pallas_call(
        functools.partial(_attn_b_kernel, tq=tq),
        grid=(batch, nt),
        in_specs=[
            pl.BlockSpec((tq, B_Q_W), lambda b, t: (b * nt + t, q_col)),
            pl.BlockSpec((blk, B_KV_W), lambda b, t: (first(b, t), k_col)),
            pl.BlockSpec((tq, B_KV_W), lambda b, t: (b * nt + t, k_col)),
            pl.BlockSpec((blk, B_KV_W), lambda b, t: (after(b, t), k_col)),
            pl.BlockSpec((B_KV_W, blk), lambda b, t: (0, first(b, t))),
            pl.BlockSpec((B_KV_W, tq), lambda b, t: (0, b * nt + t)),
            pl.BlockSpec((B_KV_W, blk), lambda b, t: (0, after(b, t))),
            pl.BlockSpec(bias.shape, lambda b, t: (0, 0, 0, 0)),
            pl.BlockSpec(sink.shape, lambda b, t: (0, 0, 0)),
        ],
        out_specs=pl.BlockSpec((B_Q_W, tq), lambda b, t: (0, b * nt + t)),
        out_shape=jax.ShapeDtypeStruct((B_Q_W, m), BF16),
        scratch_shapes=[pltpu.VMEM((tq + 2 * blk, B_KV_W), BF16), pltpu.VMEM((B_KV_W, tq + 2 * blk), BF16)],
        compiler_params=_params(("arbitrary", "arbitrary")),
        name="attn_b",
    )(p_nat, p_nat, p_nat, p_nat, vbt, vbt, vbt, bias, sink)


def _mix_ln_kernel(x_ref, oa_ref, ob_ref, ga_ref, gb_ref, wpa_ref, wpb_ref, wo_ref, g_ref, b_ref, o_ref, *, alpha):
    ya = jnp.dot(oa_ref[...], wpa_ref[...], preferred_element_type=F32)
    yb = lax.dot_general(ob_ref[...], wpb_ref[...], (((0,), (0,)), ((), ())), preferred_element_type=F32)
    merged = jax.nn.sigmoid(ga_ref[...].astype(F32)) * ya + jax.nn.sigmoid(gb_ref[...].astype(F32)) * yb
    z = jnp.dot(merged.astype(BF16), wo_ref[...], preferred_element_type=F32)
    o_ref[...] = _layer_norm(alpha * x_ref[...] + z, g_ref[...], b_ref[...])


def _mix_ln(x, oa, ob, p_nat, w_pa, w_pb, w_out, ln_g, ln_b, alpha, tm=512):
    m, dm = x.shape

    def const(arr):
        return pl.BlockSpec(arr.shape, lambda i: (0, 0), pipeline_mode=pl.Buffered(1))

    return pl.pallas_call(
        functools.partial(_mix_ln_kernel, alpha=alpha),
        grid=(m // tm,),
        in_specs=[
            pl.BlockSpec((tm, dm), lambda i: (i, 0)),
            pl.BlockSpec((tm, oa.shape[1]), lambda i: (i, 0)),
            pl.BlockSpec((ob.shape[0], tm), lambda i: (0, i)),
            pl.BlockSpec((tm, dm), lambda i: (i, NAT_GA // dm)),
            pl.BlockSpec((tm, dm), lambda i: (i, NAT_GB // dm)),
            const(w_pa), const(w_pb), const(w_out), const(ln_g), const(ln_b),
        ],
        out_specs=pl.BlockSpec((tm, dm), lambda i: (i, 0)),
        out_shape=jax.ShapeDtypeStruct((m, dm), F32),
        compiler_params=_params(("arbitrary",)),
        name="mix_ln",
    )(x, oa, ob, p_nat, p_nat, w_pa, w_pb, w_out, ln_g, ln_b)


def _ffn_ln_kernel(x_ref, wu_ref, wd_ref, g_ref, b_ref, o_ref, xb_ref, *, alpha, tf):
    f = pl.program_id(1)
    nf = pl.num_programs(1)

    @pl.when(f == 0)
    def _():
        x = x_ref[...]
        xb_ref[...] = x.astype(BF16)
        o_ref[...] = alpha * x

    sub = x_ref.shape[0] // FFN_SUBTILES
    hs = [jnp.dot(xb_ref[k * sub:(k + 1) * sub, :], wu_ref[...], preferred_element_type=F32)
          for k in range(FFN_SUBTILES)]
    for k, h in enumerate(hs):
        gate = h[:, :tf]
        up = h[:, tf:]
        act = (gate * jax.nn.sigmoid(gate) * up).astype(BF16)
        o_ref[k * sub:(k + 1) * sub, :] += jnp.dot(act, wd_ref[...], preferred_element_type=F32)

    @pl.when(f == nf - 1)
    def _():
        o_ref[...] = _layer_norm(o_ref[...], g_ref[...], b_ref[...])


def _ffn_ln(x, w_up, w_down, ln_g, ln_b, alpha, tf, tm=512):
    m, dm = x.shape
    nf = w_down.shape[0] // tf
    return pl.pallas_call(
        functools.partial(_ffn_ln_kernel, alpha=alpha, tf=tf),
        grid=(m // tm, nf),
        in_specs=[
            pl.BlockSpec((tm, dm), lambda i, f: (i, 0)),
            pl.BlockSpec((dm, 2 * tf), lambda i, f: (0, f)),
            pl.BlockSpec((tf, dm), lambda i, f: (f, 0)),
            pl.BlockSpec((1, dm), lambda i, f: (0, 0)),
            pl.BlockSpec((1, dm), lambda i, f: (0, 0)),
        ],
        out_specs=pl.BlockSpec((tm, dm), lambda i, f: (i, 0)),
        out_shape=jax.ShapeDtypeStruct((m, dm), F32),
        scratch_shapes=[pltpu.VMEM((tm, dm), BF16)],
        compiler_params=_params(("arbitrary", "arbitrary")),
        name="ffn_ln",
    )(x, w_up, w_down, ln_g, ln_b)


def _bias_tables(rel_bias):
    def toeplitz(blk, dil, cols):
        deltas = np.arange(-(2 * blk - 1), 2 * blk)
        diag = rel_bias[_t5_bucket(deltas * dil)][:, cols].T.astype(F32)
        return jnp.stack([diag[:, blk - 1 - i:4 * blk - 1 - i] for i in range(blk)], axis=1)

    tabs = []
    for g, (window, dil) in enumerate(A_GROUPS):
        heads = toeplitz(A_BLOCK, dil, slice(g * A_HEADS_PER_GROUP, (g + 1) * A_HEADS_PER_GROUP))
        masks = _band_masks(A_BLOCK, window // (2 * dil))
        tabs.append(jnp.where(masks[:, None], heads[None], NEG_INF))
    bias_a = jnp.stack(tabs)

    heads = toeplitz(B_BLOCK, 1, slice(A_HEADS, A_HEADS + B_Q_HEADS))
    masks = _band_masks(B_BLOCK, B_HALF)
    bias_b = jnp.where(masks[:, None], heads[None], NEG_INF)
    bias_b = bias_b.reshape(3, B_KV_HEADS, B_REP, B_BLOCK, 3 * B_BLOCK).transpose(0, 1, 4, 2, 3)
    bias_b = bias_b.reshape(3, B_KV_HEADS, 3 * B_BLOCK, B_REP * B_BLOCK)
    return bias_a, bias_b


def _prep_w_in(w_in):
    qa, ka, va = (w_in[:, i * A_QKV_W:(i + 1) * A_QKV_W] for i in range(3))
    o = 3 * A_QKV_W
    qb = w_in[:, o:o + B_Q_W]
    kb = w_in[:, o + B_Q_W:o + B_Q_W + B_KV_W]
    vb = w_in[:, o + B_Q_W + B_KV_W:o + B_Q_W + 2 * B_KV_W]
    o += B_Q_W + 2 * B_KV_W
    dm = w_in.shape[0]
    ga = w_in[:, o:o + dm]
    gb = w_in[:, o + dm:o + 2 * dm]

    def grp(w, g):
        return w[:, g * A_GROUP_W:(g + 1) * A_GROUP_W]

    w_nat = jnp.concatenate([ga, gb, qb, grp(qa, 0), grp(ka, 0), grp(va, 0), kb, vb], axis=1).astype(BF16)
    w_perm = [jnp.concatenate([grp(qa, g), grp(ka, g), grp(va, g)], axis=1).astype(BF16) for g in (1, 2)]
    s_nat = np.ones((1, NAT_W), np.float32)
    s_nat[:, NAT_QB:NAT_QB + B_Q_W] = QK_SCALE
    s_nat[:, NAT_Q0:NAT_Q0 + A_GROUP_W] = QK_SCALE
    s_perm = np.ones((1, 3 * A_GROUP_W), np.float32)
    s_perm[:, :A_GROUP_W] = QK_SCALE
    return w_nat, w_perm, jnp.asarray(s_nat), jnp.asarray(s_perm)


def _prep_w_up(w_up, tf):
    dm, two_ff = w_up.shape
    ff = two_ff // 2
    nf = ff // tf
    gate = w_up[:, :ff].reshape(dm, nf, tf)
    up = w_up[:, ff:].reshape(dm, nf, tf)
    return jnp.concatenate([gate, up], axis=2).reshape(dm, two_ff).astype(BF16)


def kernel(x, rel_bias, w_in, sink, w_pa, w_pb, w_out, ln1_g, ln1_b, w_up, w_down, ln2_g, ln2_b):
    batch, seq, dm = x.shape
    depth = w_in.shape[0]
    alpha = (2 * depth) ** 0.25
    tf = 512
    bias_a, bias_b = _bias_tables(rel_bias)
    h = x.reshape(batch * seq, dm)
    for l in range(depth):
        w_nat, (w4, w16), s_nat, s_perm = _prep_w_in(w_in[l])
        sink_row = jnp.repeat(sink[l].reshape(B_KV_HEADS, B_REP), B_BLOCK, axis=1)[:, None, :].astype(F32)
        p_nat, vbt = _proj_nat(h, w_nat, s_nat)
        a4, a16 = _proj_perm(h, w4, w16, s_perm, batch, seq)
        oa = _attn_a(p_nat, a4, a16, bias_a, batch, seq)
        ob = _attn_b(p_nat, vbt, bias_b, sink_row, batch, seq)
        h = _mix_ln(h, oa, ob, p_nat, w_pa[l].astype(BF16), w_pb[l].astype(BF16), w_out[l].astype(BF16),
                    ln1_g[l][None], ln1_b[l][None], alpha)
        h = _ffn_ln(h, _prep_w_up(w_up[l], tf), w_down[l].astype(BF16), ln2_g[l][None], ln2_b[l][None], alpha, tf)
    return h.reshape(batch, seq, dm)
```

```python
import functools
import math

import numpy as np
import jax
import jax.numpy as jnp
from jax import lax
from jax.experimental import pallas as pl
from jax.experimental.pallas import tpu as pltpu

HEAD_DIM = 128
A_GROUPS = ((128, 1), (512, 4), (2048, 16))
A_HEADS_PER_GROUP = 4
A_HEADS = A_HEADS_PER_GROUP * len(A_GROUPS)
A_BLOCK = 64
A_GROUP_W = A_HEADS_PER_GROUP * HEAD_DIM
A_QKV_W = A_HEADS * HEAD_DIM
B_Q_HEADS = 8
B_KV_HEADS = 2
B_REP = B_Q_HEADS // B_KV_HEADS
B_HALF = 128
B_BLOCK = 128
B_Q_W = B_Q_HEADS * HEAD_DIM
B_KV_W = B_KV_HEADS * HEAD_DIM
NUM_BUCKETS = 32
MAX_DISTANCE = 1024
LN_EPS = 1e-5
NEG_INF = -1e30
QK_SCALE = HEAD_DIM ** -0.5
A_UNROLL = 4
B_UNROLL = 2
B_ONES_ROWS = 16
MIX_SUBTILES = 4
FFN_SUBTILES = 4

BF16 = jnp.bfloat16
F32 = jnp.float32

NAT_GA, NAT_GB, NAT_QB, NAT_Q0, NAT_K0, NAT_V0, NAT_KB, NAT_VB = 0, 2048, 4096, 5120, 5632, 6144, 6656, 6912
NAT_W = 7168

VMEM_LIMIT = 56 * 1024 * 1024


def _t5_bucket(rel):
    half = NUM_BUCKETS // 2
    max_exact = half // 2
    n = np.abs(rel)
    scaled = np.log(np.maximum(n, 1) / max_exact) / math.log(MAX_DISTANCE / max_exact)
    large = np.minimum(max_exact + (scaled * (half - max_exact)).astype(np.int64), half - 1)
    return (np.where(rel > 0, half, 0) + np.where(n < max_exact, n, large)).astype(np.int32)


def _band_offsets(blk):
    qi = np.arange(blk)[:, None]
    kj = np.arange(3 * blk)[None, :]
    return kj - blk - qi


def _band_masks(blk, half):
    rel = _band_offsets(blk)
    band = np.abs(rel) <= half
    kj = np.broadcast_to(np.arange(3 * blk)[None, :], band.shape)
    return np.stack([band, band & (kj >= blk), band & (kj < 2 * blk)])


def _layer_norm(y, g, b):
    mu = jnp.mean(y, axis=-1, keepdims=True)
    yc = y - mu
    var = jnp.mean(yc * yc, axis=-1, keepdims=True)
    return yc * lax.rsqrt(var + LN_EPS) * g + b


def _params(sem):
    return pltpu.CompilerParams(dimension_semantics=sem, vmem_limit_bytes=VMEM_LIMIT)


def _proj_nat_kernel(x_ref, w_ref, s_ref, o_ref, vt_ref, xb_ref):
    j = pl.program_id(1)

    @pl.when(j == 0)
    def _():
        xb_ref[...] = x_ref[...].astype(BF16)

    res = jnp.dot(xb_ref[...], w_ref[...], preferred_element_type=F32) * s_ref[...]
    o_ref[...] = res.astype(o_ref.dtype)

    @pl.when(j == pl.num_programs(1) - 1)
    def _():
        vt_ref[...] = res[:, res.shape[1] - B_KV_W:].T.astype(vt_ref.dtype)


def _proj_nat(x, w, colscale, tm=1024, tn=1792):
    m, k = x.shape
    n = w.shape[1]
    return pl.pallas_call(
        _proj_nat_kernel,
        grid=(m // tm, n // tn),
        in_specs=[
            pl.BlockSpec((tm, k), lambda i, j: (i, 0)),
            pl.BlockSpec((k, tn), lambda i, j: (0, j)),
            pl.BlockSpec((1, tn), lambda i, j: (0, j)),
        ],
        out_specs=[pl.BlockSpec((tm, tn), lambda i, j: (i, j)),
                   pl.BlockSpec((B_KV_W, tm), lambda i, j: (0, i))],
        out_shape=[jax.ShapeDtypeStruct((m, n), BF16), jax.ShapeDtypeStruct((B_KV_W, m), BF16)],
        scratch_shapes=[pltpu.VMEM((tm, k), BF16)],
        compiler_params=_params(("arbitrary", "arbitrary")),
        name="proj_nat",
    )(x, w, colscale)


def _proj_perm_kernel(x_ref, w4_ref, w16_ref, s_ref, o4_ref, o16_ref, xb_ref, res_ref, *, tm):
    xb_ref[...] = x_ref[...].astype(BF16)
    ncb = res_ref.shape[0]
    for d, w_ref, o_ref in ((4, w4_ref, o4_ref), (16, w16_ref, o16_ref)):
        n = tm // d
        res = jnp.dot(xb_ref[...], w_ref[...], preferred_element_type=F32) * s_ref[...]
        for cb in range(ncb):
            res_ref[cb] = res[:, cb * HEAD_DIM:(cb + 1) * HEAD_DIM]
        for r in range(d):
            for cb in range(ncb):
                o_ref[0, r, :, cb * HEAD_DIM:(cb + 1) * HEAD_DIM] = (
                    res_ref[cb, pl.ds(r, n, stride=d), :].astype(BF16))


def _proj_perm(x, w4, w16, colscale, batch, seq, tm=1024):
    m, k = x.shape
    n = w4.shape[1]
    nt = seq // tm
    outs = []
    out_specs = []
    for d in (4, 16):
        outs.append(jax.ShapeDtypeStruct((batch, d, seq // d, n), BF16))
        out_specs.append(pl.BlockSpec((1, d, tm // d, n), lambda i: (i // nt, 0, i % nt, 0)))
    return pl.pallas_call(
        functools.partial(_proj_perm_kernel, tm=tm),
        grid=(m // tm,),
        in_specs=[
            pl.BlockSpec((tm, k), lambda i: (i, 0)),
            pl.BlockSpec((k, n), lambda i: (0, 0)),
            pl.BlockSpec((k, n), lambda i: (0, 0)),
            pl.BlockSpec((1, n), lambda i: (0, 0)),
        ],
        out_specs=out_specs,
        out_shape=outs,
        scratch_shapes=[pltpu.VMEM((tm, k), BF16), pltpu.VMEM((n // HEAD_DIM, tm, HEAD_DIM), F32)],
        compiler_params=_params(("arbitrary",)),
        name="proj_perm",
    )(x, w4, w16, colscale)


def _band_block(q, kw, vw, bias, sink=None):
    s = lax.dot_general(q, kw, (((1,), (1,)), ((), ())), preferred_element_type=F32) + bias
    m = jnp.max(s, axis=-1, keepdims=True)
    if sink is not None:
        m = jnp.maximum(m, sink)
    p = jnp.exp(s - m)
    den = jnp.sum(p, axis=-1, keepdims=True)
    if sink is not None:
        den = den + jnp.exp(sink - m)
    acc = jnp.dot(p.astype(BF16), vw, preferred_element_type=F32)
    return acc, m, den


def _edge_variant(c, nblk, t, nt):
    first = jnp.logical_and(c == 0, t == 0)
    last = jnp.logical_and(c == nblk - 1, t == nt - 1)
    return jnp.where(first, 1, jnp.where(last, 2, 0))


def _attn_a_kernel(*refs, tq):
    ins = refs[:21]
    bias_ref = refs[21]
    o_ref = refs[22]
    og_ref, lse_ref = refs[23:25]
    win_refs = refs[25:29]
    t = pl.program_id(1)
    nt = pl.num_programs(1)
    blk = A_BLOCK

    for g, (_, d) in enumerate(A_GROUPS):
        q_ref, kp_ref, k_ref, kn_ref, vp_ref, v_ref, vn_ref = ins[7 * g:7 * g + 7]
        n = tq // d
        nblk = n // blk

        if nblk > 1:
            kw_ref, vw_ref = win_refs[2 * g:2 * g + 2]
            for w_ref, p_ref, c_ref, n_ref in ((kw_ref, kp_ref, k_ref, kn_ref), (vw_ref, vp_ref, v_ref, vn_ref)):
                w_ref[:, 0:blk, :] = p_ref[...].reshape(d, blk, A_GROUP_W)
                w_ref[:, blk:blk + n, :] = c_ref[...].reshape(d, n, A_GROUP_W)
                w_ref[:, blk + n:, :] = n_ref[...].reshape(d, blk, A_GROUP_W)

        def block(it, carry, g=g, d=d, nblk=nblk):
            qs, kws, vws, biases, dsts = [], [], [], [], []
            for j in range(A_UNROLL):
                idx = it * A_UNROLL + j
                r = idx // nblk
                c = idx % nblk
                variant = _edge_variant(c, nblk, t, nt)
                qrows = pl.ds(pl.multiple_of(c * blk, blk), blk)
                wrows = pl.ds(pl.multiple_of(c * blk, blk), 3 * blk)
                if g == 0:
                    q = q_ref[qrows, :]
                else:
                    q = q_ref[0, r, qrows, :]
                if nblk > 1:
                    kw = kw_ref[r, wrows, :]
                    vw = vw_ref[r, wrows, :]
                else:
                    kw = jnp.concatenate([kp_ref[0, r], k_ref[0, r], kn_ref[0, r]], axis=0)
                    vw = jnp.concatenate([vp_ref[0, r], v_ref[0, r], vn_ref[0, r]], axis=0)
                if d == 1:
                    dst = qrows
                else:
                    dst = pl.ds(r + d * blk * c, blk, stride=d)
                for h in range(A_HEADS_PER_GROUP):
                    cols = slice(h * HEAD_DIM, (h + 1) * HEAD_DIM)
                    qs.append(q[:, cols])
                    kws.append(kw[:, cols])
                    vws.append(vw[:, cols])
                    biases.append(bias_ref[g, variant, h])
                    dsts.append((h, dst))
            q3 = jnp.stack(qs)
            kw3 = jnp.stack(kws)
            vw3 = jnp.stack(vws)
            s = jnp.einsum("uqd,ukd->uqk", q3, kw3, preferred_element_type=F32) + jnp.stack(biases)
            m = jnp.max(s, axis=-1, keepdims=True)
            p = jnp.exp(s - m).astype(BF16)
            vx = jnp.concatenate([vw3, jnp.ones_like(vw3)], axis=-1)
            acc = jnp.einsum("uqk,ukd->uqd", p, vx, preferred_element_type=F32)
            num = acc[..., :HEAD_DIM]
            den = acc[..., HEAD_DIM:]
            og = num / den
            lse = m + jnp.log(den)
            for u, (h, dst) in enumerate(dsts):
                og_ref[g, h, dst, :] = og[u]
                lse_ref[g, h, dst, :] = lse[u]
            return carry

        lax.fori_loop(0, tq // blk // A_UNROLL, block, 0)

    chunk = 128

    def merge(i, carry):
        rs = pl.ds(pl.multiple_of(i * chunk, chunk), chunk)
        for h in range(A_HEADS_PER_GROUP):
            ls = [lse_ref[g, h, rs, :] for g in range(3)]
            mx = jnp.maximum(jnp.maximum(ls[0], ls[1]), ls[2])
            num = jnp.zeros((chunk, HEAD_DIM), F32)
            dn = jnp.zeros((chunk, HEAD_DIM), F32)
            for g in range(3):
                e = jnp.exp(ls[g] - mx)
                num = num + e * og_ref[g, h, rs, :]
                dn = dn + e
            o_ref[rs, h * HEAD_DIM:(h + 1) * HEAD_DIM] = (num / dn).astype(o_ref.dtype)
        return carry

    lax.fori_loop(0, tq // chunk, merge, 0)


def _attn_a(p_nat, a4, a16, bias, batch, seq, tq=1024):
    m = p_nat.shape[0]
    nt = seq // tq
    blk = A_BLOCK
    gw = A_GROUP_W
    in_specs = []
    args = []

    nb_tile = tq // blk
    last64 = m // blk - 1
    q_col, k_col, v_col = NAT_Q0 // gw, NAT_K0 // gw, NAT_V0 // gw

    def g0_main(col):
        return pl.BlockSpec((tq, gw), lambda b, t: (b * nt + t, col))

    def g0_prev(col):
        return pl.BlockSpec((blk, gw), lambda b, t: (jnp.maximum((b * nt + t) * nb_tile - 1, 0), col))

    def g0_next(col):
        return pl.BlockSpec((blk, gw), lambda b, t: (jnp.minimum((b * nt + t + 1) * nb_tile, last64), col))

    in_specs += [g0_main(q_col), g0_prev(k_col), g0_main(k_col), g0_next(k_col),
                 g0_prev(v_col), g0_main(v_col), g0_next(v_col)]
    args += [p_nat] * 7

    for arr, d in ((a4, 4), (a16, 16)):
        n = tq // d
        nb = n // blk
        lastb = seq // d // blk - 1

        def main(col, d=d, n=n):
            return pl.BlockSpec((1, d, n, gw), lambda b, t: (b, 0, t, col))

        def prev(col, d=d, nb=nb):
            return pl.BlockSpec((1, d, blk, gw), lambda b, t: (b, 0, jnp.maximum(t * nb - 1, 0), col))

        def nxt(col, d=d, nb=nb, lastb=lastb):
            return pl.BlockSpec((1, d, blk, gw), lambda b, t: (b, 0, jnp.minimum((t + 1) * nb, lastb), col))

        in_specs += [main(0), prev(1), main(1), nxt(1), prev(2), main(2), nxt(2)]
        args += [arr] * 7

    in_specs.append(pl.BlockSpec(bias.shape, lambda b, t: (0,) * bias.ndim))
    args.append(bias)

    return pl.pallas_call(
        functools.partial(_attn_a_kernel, tq=tq),
        grid=(batch, nt),
        in_specs=in_specs,
        out_specs=pl.BlockSpec((tq, gw), lambda b, t: (b * nt + t, 0)),
        out_shape=jax.ShapeDtypeStruct((m, gw), BF16),
        scratch_shapes=(
            [pltpu.VMEM((3, A_HEADS_PER_GROUP, tq, HEAD_DIM), F32)] * 2
            + [pltpu.VMEM((d, tq // d + 2 * blk, gw), BF16) for _, d in A_GROUPS[:2] for _ in range(2)]),
        compiler_params=_params(("arbitrary", "arbitrary")),
        name="attn_a",
    )(*args)


def _attn_b_kernel(q_ref, kp_ref, k_ref, kn_ref, vp_ref, v_ref, vn_ref, bias_ref, sink_ref, o_ref,
                   kw_ref, vw_ref, *, tq):
    t = pl.program_id(1)
    nt = pl.num_programs(1)
    blk = B_BLOCK
    nblk = tq // blk

    kw_ref[0:blk, :] = kp_ref[...]
    kw_ref[blk:blk + tq, :] = k_ref[...]
    kw_ref[blk + tq:, :] = kn_ref[...]
    vw_ref[:, 0:blk] = vp_ref[...]
    vw_ref[:, blk:blk + tq] = v_ref[...]
    vw_ref[:, blk + tq:] = vn_ref[...]
    ones = jnp.ones((B_ONES_ROWS, 3 * blk), BF16)

    def block(it, carry):
        units = []
        for j in range(B_UNROLL):
            c = it * B_UNROLL + j
            variant = _edge_variant(c, nblk, t, nt)
            rows = pl.ds(pl.multiple_of(c * blk, blk), blk)
            wrows = pl.ds(pl.multiple_of(c * blk, blk), 3 * blk)
            for kvh in range(B_KV_HEADS):
                units.append((kvh, variant, rows, wrows))
        scores = []
        for kvh, variant, rows, wrows in units:
            kcols = slice(kvh * HEAD_DIM, (kvh + 1) * HEAD_DIM)
            q4 = jnp.concatenate(
                [q_ref[rows, (kvh * B_REP + rep) * HEAD_DIM:(kvh * B_REP + rep + 1) * HEAD_DIM]
                 for rep in range(B_REP)], axis=0)
            scores.append(lax.dot_general(kw_ref[wrows, kcols], q4, (((1,), (1,)), ((), ())),
                                          preferred_element_type=F32))
        probs = []
        for (kvh, variant, rows, wrows), s in zip(units, scores):
            s = s + bias_ref[variant, kvh]
            sink = sink_ref[kvh]
            m = jnp.maximum(jnp.max(s, axis=0, keepdims=True), sink)
            probs.append((jnp.exp(s - m).astype(BF16), jnp.exp(sink - m)))
        for (kvh, variant, rows, wrows), (p, psink) in zip(units, probs):
            kcols = slice(kvh * HEAD_DIM, (kvh + 1) * HEAD_DIM)
            vx = jnp.concatenate([vw_ref[kcols, wrows], ones], axis=0)
            acc = jnp.dot(vx, p, preferred_element_type=F32)
            den = acc[HEAD_DIM:HEAD_DIM + 1, :] + psink
            o = (acc[:HEAD_DIM, :] / den).astype(o_ref.dtype)
            for rep in range(B_REP):
                h = kvh * B_REP + rep
                o_ref[h * HEAD_DIM:(h + 1) * HEAD_DIM, rows] = o[:, rep * blk:(rep + 1) * blk]
        return carry

    lax.fori_loop(0, nblk // B_UNROLL, block, 0)


def _attn_b(p_nat, vbt, bias, sink, batch, seq, tq=1024):
    m = p_nat.shape[0]
    nt = seq // tq
    blk = B_BLOCK
    nb_tile = tq // blk
    lastb = m // blk - 1
    q_col = NAT_QB // B_Q_W
    k_col = NAT_KB // B_KV_W

    def first(b, t):
        return jnp.maximum((b * nt + t) * nb_tile - 1, 0)

    def after(b, t):
        return jnp.minimum((b * nt + t + 1) * nb_tile, lastb)

    return pl.pallas_call(
        functools.partial(_attn_b_kernel, tq=tq),
        grid=(batch, nt),
        in_specs=[
            pl.BlockSpec((tq, B_Q_W), lambda b, t: (b * nt + t, q_col)),
            pl.BlockSpec((blk, B_KV_W), lambda b, t: (first(b, t), k_col)),
            pl.BlockSpec((tq, B_KV_W), lambda b, t: (b * nt + t, k_col)),
            pl.BlockSpec((blk, B_KV_W), lambda b, t: (after(b, t), k_col)),
            pl.BlockSpec((B_KV_W, blk), lambda b, t: (0, first(b, t))),
            pl.BlockSpec((B_KV_W, tq), lambda b, t: (0, b * nt + t)),
            pl.BlockSpec((B_KV_W, blk), lambda b, t: (0, after(b, t))),
            pl.BlockSpec(bias.shape, lambda b, t: (0, 0, 0, 0)),
            pl.BlockSpec(sink.shape, lambda b, t: (0, 0, 0)),
        ],
        out_specs=pl.BlockSpec((B_Q_W, tq), lambda b, t: (0, b * nt + t)),
        out_shape=jax.ShapeDtypeStruct((B_Q_W, m), BF16),
        scratch_shapes=[pltpu.VMEM((tq + 2 * blk, B_KV_W), BF16), pltpu.VMEM((B_KV_W, tq + 2 * blk), BF16)],
        compiler_params=_params(("arbitrary", "arbitrary")),
        name="attn_b",
    )(p_nat, p_nat, p_nat, p_nat, vbt, vbt, vbt, bias, sink)


def _mix_ln_kernel(x_ref, oa_ref, ob_ref, ga_ref, gb_ref, wpa_ref, wpb_ref, wo_ref, g_ref, b_ref, o_ref, *, alpha):
    sub = x_ref.shape[0] // MIX_SUBTILES
    ys = []
    for k in range(MIX_SUBTILES):
        rows = slice(k * sub, (k + 1) * sub)
        ya = jnp.dot(oa_ref[rows, :], wpa_ref[...], preferred_element_type=F32)
        yb = lax.dot_general(ob_ref[:, rows], wpb_ref[...], (((0,), (0,)), ((), ())),
                             preferred_element_type=F32)
        ys.append((ya, yb))
    for k, (ya, yb) in enumerate(ys):
        rows = slice(k * sub, (k + 1) * sub)
        merged = (jax.nn.sigmoid(ga_ref[rows, :].astype(F32)) * ya
                  + jax.nn.sigmoid(gb_ref[rows, :].astype(F32)) * yb)
        z = jnp.dot(merged.astype(BF16), wo_ref[...], preferred_element_type=F32)
        o_ref[rows, :] = _layer_norm(alpha * x_ref[rows, :] + z, g_ref[...], b_ref[...])


def _mix_ln(x, oa, ob, p_nat, w_pa, w_pb, w_out, ln_g, ln_b, alpha, tm=512):
    m, dm = x.shape

    def const(arr):
        return pl.BlockSpec(arr.shape, lambda i: (0, 0), pipeline_mode=pl.Buffered(1))

    return pl.pallas_call(
        functools.partial(_mix_ln_kernel, alpha=alpha),
        grid=(m // tm,),
        in_specs=[
            pl.BlockSpec((tm, dm), lambda i: (i, 0)),
            pl.BlockSpec((tm, oa.shape[1]), lambda i: (i, 0)),
            pl.BlockSpec((ob.shape[0], tm), lambda i: (0, i)),
            pl.BlockSpec((tm, dm), lambda i: (i, NAT_GA // dm)),
            pl.BlockSpec((tm, dm), lambda i: (i, NAT_GB // dm)),
            const(w_pa), const(w_pb), const(w_out), const(ln_g), const(ln_b),
        ],
        out_specs=pl.BlockSpec((tm, dm), lambda i: (i, 0)),
        out_shape=jax.ShapeDtypeStruct((m, dm), F32),
        compiler_params=_params(("arbitrary",)),
        name="mix_ln",
    )(x, oa, ob, p_nat, p_nat, w_pa, w_pb, w_out, ln_g, ln_b)


def _ffn_step(x_ref, wg_ref, wu_ref, wd_ref, g_ref, b_ref, o_ref, alpha, first, last):
    sub = x_ref.shape[0] // FFN_SUBTILES
    hs = []
    for k in range(FFN_SUBTILES):
        xb = x_ref[k * sub:(k + 1) * sub, :].astype(BF16)
        hs.append((jnp.dot(xb, wg_ref[...], preferred_element_type=F32),
                   jnp.dot(xb, wu_ref[...], preferred_element_type=F32)))
    for k, (gate, up) in enumerate(hs):
        rows = slice(k * sub, (k + 1) * sub)
        act = (gate * jax.nn.sigmoid(gate) * up).astype(BF16)
        base = alpha * x_ref[rows, :] if first else o_ref[rows, :]
        y = base + jnp.dot(act, wd_ref[...], preferred_element_type=F32)
        o_ref[rows, :] = _layer_norm(y, g_ref[...], b_ref[...]) if last else y


def _ffn_ln_kernel(x_ref, wg_ref, wu_ref, wd_ref, g_ref, b_ref, o_ref, *, alpha):
    f = pl.program_id(1)
    nf = pl.num_programs(1)
    step = functools.partial(_ffn_step, x_ref, wg_ref, wu_ref, wd_ref, g_ref, b_ref, o_ref, alpha)
    pl.when(f == 0)(functools.partial(step, True, False))
    pl.when(jnp.logical_and(f > 0, f < nf - 1))(functools.partial(step, False, False))
    pl.when(f == nf - 1)(functools.partial(step, False, True))


def _ffn_ln(x, w_up, w_down, ln_g, ln_b, alpha, tf=512, tm=1024):
    m, dm = x.shape
    nf = w_down.shape[0] // tf
    return pl.pallas_call(
        functools.partial(_ffn_ln_kernel, alpha=alpha),
        grid=(m // tm, nf),
        in_specs=[
            pl.BlockSpec((tm, dm), lambda i, f: (i, 0)),
            pl.BlockSpec((dm, tf), lambda i, f: (0, f)),
            pl.BlockSpec((dm, tf), lambda i, f: (0, nf + f)),
            pl.BlockSpec((tf, dm), lambda i, f: (f, 0)),
            pl.BlockSpec((1, dm), lambda i, f: (0, 0)),
            pl.BlockSpec((1, dm), lambda i, f: (0, 0)),
        ],
        out_specs=pl.BlockSpec((tm, dm), lambda i, f: (i, 0)),
        out_shape=jax.ShapeDtypeStruct((m, dm), F32),
        compiler_params=_params(("arbitrary", "arbitrary")),
        name="ffn_ln",
    )(x, w_up, w_up, w_down, ln_g, ln_b)


def _bias_tables(rel_bias):
    def toeplitz(blk, dil, cols):
        deltas = np.arange(-(2 * blk - 1), 2 * blk)
        diag = rel_bias[_t5_bucket(deltas * dil)][:, cols].T.astype(F32)
        return jnp.stack([diag[:, blk - 1 - i:4 * blk - 1 - i] for i in range(blk)], axis=1)

    tabs = []
    for g, (window, dil) in enumerate(A_GROUPS):
        heads = toeplitz(A_BLOCK, dil, slice(g * A_HEADS_PER_GROUP, (g + 1) * A_HEADS_PER_GROUP))
        masks = _band_masks(A_BLOCK, window // (2 * dil))
        tabs.append(jnp.where(masks[:, None], heads[None], NEG_INF))
    bias_a = jnp.stack(tabs)

    heads = toeplitz(B_BLOCK, 1, slice(A_HEADS, A_HEADS + B_Q_HEADS))
    masks = _band_masks(B_BLOCK, B_HALF)
    bias_b = jnp.where(masks[:, None], heads[None], NEG_INF)
    bias_b = bias_b.reshape(3, B_KV_HEADS, B_REP, B_BLOCK, 3 * B_BLOCK).transpose(0, 1, 4, 2, 3)
    bias_b = bias_b.reshape(3, B_KV_HEADS, 3 * B_BLOCK, B_REP * B_BLOCK)
    return bias_a, bias_b


def _prep_w_in(w_in):
    qa, ka, va = (w_in[:, i * A_QKV_W:(i + 1) * A_QKV_W] for i in range(3))
    o = 3 * A_QKV_W
    qb = w_in[:, o:o + B_Q_W]
    kb = w_in[:, o + B_Q_W:o + B_Q_W + B_KV_W]
    vb = w_in[:, o + B_Q_W + B_KV_W:o + B_Q_W + 2 * B_KV_W]
    o += B_Q_W + 2 * B_KV_W
    dm = w_in.shape[0]
    ga = w_in[:, o:o + dm]
    gb = w_in[:, o + dm:o + 2 * dm]

    def grp(w, g):
        return w[:, g * A_GROUP_W:(g + 1) * A_GROUP_W]

    w_nat = jnp.concatenate([ga, gb, qb, grp(qa, 0), grp(ka, 0), grp(va, 0), kb, vb], axis=1).astype(BF16)
    w_perm = [jnp.concatenate([grp(qa, g), grp(ka, g), grp(va, g)], axis=1).astype(BF16) for g in (1, 2)]
    s_nat = np.ones((1, NAT_W), np.float32)
    s_nat[:, NAT_QB:NAT_QB + B_Q_W] = QK_SCALE
    s_nat[:, NAT_Q0:NAT_Q0 + A_GROUP_W] = QK_SCALE
    s_perm = np.ones((1, 3 * A_GROUP_W), np.float32)
    s_perm[:, :A_GROUP_W] = QK_SCALE
    return w_nat, w_perm, jnp.asarray(s_nat), jnp.asarray(s_perm)


def kernel(x, rel_bias, w_in, sink, w_pa, w_pb, w_out, ln1_g, ln1_b, w_up, w_down, ln2_g, ln2_b):
    batch, seq, dm = x.shape
    depth = w_in.shape[0]
    alpha = (2 * depth) ** 0.25
    bias_a, bias_b = _bias_tables(rel_bias)
    h = x.reshape(batch * seq, dm)
    for l in range(depth):
        w_nat, (w4, w16), s_nat, s_perm = _prep_w_in(w_in[l])
        sink_row = jnp.repeat(sink[l].reshape(B_KV_HEADS, B_REP), B_BLOCK, axis=1)[:, None, :].astype(F32)
        p_nat, vbt = _proj_nat(h, w_nat, s_nat)
        a4, a16 = _proj_perm(h, w4, w16, s_perm, batch, seq)
        oa = _attn_a(p_nat, a4, a16, bias_a, batch, seq)
        ob = _attn_b(p_nat, vbt, bias_b, sink_row, batch, seq)
        h = _mix_ln(h, oa, ob, p_nat, w_pa[l].astype(BF16), w_pb[l].astype(BF16), w_out[l].astype(BF16),
                    ln1_g[l][None], ln1_b[l][None], alpha)
        h = _ffn_ln(h, w_up[l].astype(BF16), w_down[l].astype(BF16), ln2_g[l][None], ln2_b[l][None], alpha)
    return h.reshape(batch, seq, dm)
```

```python
import functools
import math

import numpy as np
import jax
import jax.numpy as jnp
from jax import lax
from jax.experimental import pallas as pl
from jax.experimental.pallas import tpu as pltpu

HEAD_DIM = 128
A_GROUPS = ((128, 1), (512, 4), (2048, 16))
A_HEADS_PER_GROUP = 4
A_HEADS = A_HEADS_PER_GROUP * len(A_GROUPS)
A_BLOCK = 64
A_GROUP_W = A_HEADS_PER_GROUP * HEAD_DIM
A_QKV_W = A_HEADS * HEAD_DIM
B_Q_HEADS = 8
B_KV_HEADS = 2
B_REP = B_Q_HEADS // B_KV_HEADS
B_HALF = 128
B_BLOCK = 128
B_Q_W = B_Q_HEADS * HEAD_DIM
B_KV_W = B_KV_HEADS * HEAD_DIM
NUM_BUCKETS = 32
MAX_DISTANCE = 1024
LN_EPS = 1e-5
NEG_INF = -1e30
QK_SCALE = HEAD_DIM ** -0.5
A_QROWS = (128, 128, 64)
A_UNROLL_ROWS = 256
B_UNROLL = 2
B_ONES_ROWS = 16
MIX_SUBTILES = 4
FFN_SUBTILES = 4

BF16 = jnp.bfloat16
F32 = jnp.float32

NAT_GA, NAT_GB, NAT_QB, NAT_Q0, NAT_K0, NAT_V0, NAT_KB, NAT_VB = 0, 2048, 4096, 5120, 5632, 6144, 6656, 6912
NAT_W = 7168

VMEM_LIMIT = 56 * 1024 * 1024


def _t5_bucket(rel):
    half = NUM_BUCKETS // 2
    max_exact = half // 2
    n = np.abs(rel)
    scaled = np.log(np.maximum(n, 1) / max_exact) / math.log(MAX_DISTANCE / max_exact)
    large = np.minimum(max_exact + (scaled * (half - max_exact)).astype(np.int64), half - 1)
    return (np.where(rel > 0, half, 0) + np.where(n < max_exact, n, large)).astype(np.int32)


def _band_offsets(blk):
    qi = np.arange(blk)[:, None]
    kj = np.arange(3 * blk)[None, :]
    return kj - blk - qi


def _band_masks(rows, halo, half):
    kj = np.arange(rows + 2 * halo)[None, :]
    band = np.abs(kj - halo - np.arange(rows)[:, None]) <= half
    return np.stack([band, band & (kj >= halo), band & (kj < rows + halo)])


def _layer_norm(y, g, b):
    mu = jnp.mean(y, axis=-1, keepdims=True)
    yc = y - mu
    var = jnp.mean(yc * yc, axis=-1, keepdims=True)
    return yc * lax.rsqrt(var + LN_EPS) * g + b


def _params(sem):
    return pltpu.CompilerParams(dimension_semantics=sem, vmem_limit_bytes=VMEM_LIMIT)


def _proj_nat_kernel(x_ref, w_ref, s_ref, o_ref, vt_ref, xb_ref):
    j = pl.program_id(1)

    @pl.when(j == 0)
    def _():
        xb_ref[...] = x_ref[...].astype(BF16)

    res = jnp.dot(xb_ref[...], w_ref[...], preferred_element_type=F32) * s_ref[...]
    o_ref[...] = res.astype(o_ref.dtype)

    @pl.when(j == pl.num_programs(1) - 1)
    def _():
        vt_ref[...] = res[:, res.shape[1] - B_KV_W:].T.astype(vt_ref.dtype)


def _proj_nat(x, w, colscale, tm=1024, tn=1792):
    m, k = x.shape
    n = w.shape[1]
    return pl.pallas_call(
        _proj_nat_kernel,
        grid=(m // tm, n // tn),
        in_specs=[
            pl.BlockSpec((tm, k), lambda i, j: (i, 0)),
            pl.BlockSpec((k, tn), lambda i, j: (0, j)),
            pl.BlockSpec((1, tn), lambda i, j: (0, j)),
        ],
        out_specs=[pl.BlockSpec((tm, tn), lambda i, j: (i, j)),
                   pl.BlockSpec((B_KV_W, tm), lambda i, j: (0, i)),
                   pl.BlockSpec((tm, k), lambda i, j: (i, 0))],
        out_shape=[jax.ShapeDtypeStruct((m, n), BF16), jax.ShapeDtypeStruct((B_KV_W, m), BF16),
                   jax.ShapeDtypeStruct((m, k), BF16)],
        compiler_params=_params(("arbitrary", "arbitrary")),
        name="proj_nat",
    )(x, w, colscale)


def _proj_perm_kernel(xb_ref, w4_ref, w16_ref, o4_ref, o16_ref, res_ref, tmp_ref, *, tm):
    gw = A_GROUP_W
    ncb = gw // HEAD_DIM
    q4 = tm // 4
    for g, (d, w_ref, o_ref) in enumerate(((4, w4_ref, o4_ref), (16, w16_ref, o16_ref))):
        for part in range(3):
            slot = 3 * g + part
            cols = slice(part * gw, (part + 1) * gw)
            res = jnp.dot(xb_ref[...], w_ref[:, cols], preferred_element_type=F32)
            if part == 0:
                res = res * QK_SCALE
            for cb in range(ncb):
                res_ref[slot, cb] = res[:, cb * HEAD_DIM:(cb + 1) * HEAD_DIM]
            for cb in range(ncb):
                ocols = slice(part * gw + cb * HEAD_DIM, part * gw + (cb + 1) * HEAD_DIM)
                if d == 4:
                    for r in range(4):
                        o_ref[0, r, :, ocols] = res_ref[slot, cb, pl.ds(r, q4, stride=4), :].astype(BF16)
                else:
                    for c1 in range(4):
                        tmp_ref[part, cb, c1 * q4:(c1 + 1) * q4, :] = res_ref[slot, cb, pl.ds(c1, q4, stride=4), :]
                    for c1 in range(4):
                        for c2 in range(4):
                            o_ref[0, c1 + 4 * c2, :, ocols] = (
                                tmp_ref[part, cb, pl.ds(c1 * q4 + c2, q4 // 4, stride=4), :].astype(BF16))


def _proj_perm(xb, w4, w16, batch, seq, tm=1024):
    m, k = xb.shape
    n = w4.shape[1]
    nt = seq // tm
    outs = []
    out_specs = []
    for d in (4, 16):
        outs.append(jax.ShapeDtypeStruct((batch, d, seq // d, n), BF16))
        out_specs.append(pl.BlockSpec((1, d, tm // d, n), lambda i: (i // nt, 0, i % nt, 0)))
    return pl.pallas_call(
        functools.partial(_proj_perm_kernel, tm=tm),
        grid=(m // tm,),
        in_specs=[
            pl.BlockSpec((tm, k), lambda i: (i, 0)),
            pl.BlockSpec((k, n), lambda i: (0, 0), pipeline_mode=pl.Buffered(1)),
            pl.BlockSpec((k, n), lambda i: (0, 0), pipeline_mode=pl.Buffered(1)),
        ],
        out_specs=out_specs,
        out_shape=outs,
        scratch_shapes=[pltpu.VMEM((6, A_GROUP_W // HEAD_DIM, tm, HEAD_DIM), F32),
                        pltpu.VMEM((3, A_GROUP_W // HEAD_DIM, tm, HEAD_DIM), F32)],
        compiler_params=_params(("arbitrary",)),
        name="proj_perm",
    )(xb, w4, w16)


def _band_block(q, kw, vw, bias, sink=None):
    s = lax.dot_general(q, kw, (((1,), (1,)), ((), ())), preferred_element_type=F32) + bias
    m = jnp.max(s, axis=-1, keepdims=True)
    if sink is not None:
        m = jnp.maximum(m, sink)
    p = jnp.exp(s - m)
    den = jnp.sum(p, axis=-1, keepdims=True)
    if sink is not None:
        den = den + jnp.exp(sink - m)
    acc = jnp.dot(p.astype(BF16), vw, preferred_element_type=F32)
    return acc, m, den


def _edge_variant(c, nblk, t, nt):
    first = jnp.logical_and(c == 0, t == 0)
    last = jnp.logical_and(c == nblk - 1, t == nt - 1)
    return jnp.where(first, 1, jnp.where(last, 2, 0))


def _attn_a_kernel(*refs, tq):
    ins = refs[:21]
    bias_refs = refs[21:24]
    o_ref = refs[24]
    og_ref, lse_ref = refs[25:27]
    win_refs = refs[27:31]
    t = pl.program_id(1)
    nt = pl.num_programs(1)
    blk = A_BLOCK

    for g, (_, d) in enumerate(A_GROUPS):
        q_ref, kp_ref, k_ref, kn_ref, vp_ref, v_ref, vn_ref = ins[7 * g:7 * g + 7]
        bias_ref = bias_refs[g]
        n = tq // d
        qr = A_QROWS[g]
        nblk = n // qr
        unroll = A_UNROLL_ROWS // qr

        if n > blk:
            kw_ref, vw_ref = win_refs[2 * g:2 * g + 2]
            for w_ref, p_ref, c_ref, n_ref in ((kw_ref, kp_ref, k_ref, kn_ref), (vw_ref, vp_ref, v_ref, vn_ref)):
                w_ref[:, 0:blk, :] = p_ref[...].reshape(d, blk, A_GROUP_W)
                w_ref[:, blk:blk + n, :] = c_ref[...].reshape(d, n, A_GROUP_W)
                w_ref[:, blk + n:, :] = n_ref[...].reshape(d, blk, A_GROUP_W)

        def block(it, carry, g=g, d=d, n=n, qr=qr, nblk=nblk, unroll=unroll, bias_ref=bias_ref):
            qs, kws, vws, biases, dsts = [], [], [], [], []
            for j in range(unroll):
                idx = it * unroll + j
                r = idx // nblk
                c = idx % nblk
                variant = _edge_variant(c, nblk, t, nt)
                qrows = pl.ds(pl.multiple_of(c * qr, qr), qr)
                wrows = pl.ds(pl.multiple_of(c * qr, qr), qr + 2 * blk)
                if g == 0:
                    q = q_ref[qrows, :]
                else:
                    q = q_ref[0, r, qrows, :]
                if n > blk:
                    kw = kw_ref[r, wrows, :]
                    vw = vw_ref[r, wrows, :]
                else:
                    kw = jnp.concatenate([kp_ref[0, r], k_ref[0, r], kn_ref[0, r]], axis=0)
                    vw = jnp.concatenate([vp_ref[0, r], v_ref[0, r], vn_ref[0, r]], axis=0)
                if d == 1:
                    dst = qrows
                else:
                    dst = pl.ds(r + d * qr * c, qr, stride=d)
                for h in range(A_HEADS_PER_GROUP):
                    cols = slice(h * HEAD_DIM, (h + 1) * HEAD_DIM)
                    qs.append(q[:, cols])
                    kws.append(kw[:, cols])
                    vws.append(vw[:, cols])
                    biases.append(bias_ref[variant, h])
                    dsts.append((h, dst))
            q3 = jnp.stack(qs)
            kw3 = jnp.stack(kws)
            vw3 = jnp.stack(vws)
            s = jnp.einsum("uqd,ukd->uqk", q3, kw3, preferred_element_type=F32) + jnp.stack(biases)
            m = jnp.max(s, axis=-1, keepdims=True)
            p = jnp.exp(s - m).astype(BF16)
            vx = jnp.concatenate([vw3, jnp.ones_like(vw3)], axis=-1)
            acc = jnp.einsum("uqk,ukd->uqd", p, vx, preferred_element_type=F32)
            num = acc[..., :HEAD_DIM]
            den = acc[..., HEAD_DIM:]
            og = num / den
            lse = m + jnp.log(den)
            for u, (h, dst) in enumerate(dsts):
                og_ref[g, h, dst, :] = og[u]
                lse_ref[g, h, dst, :] = lse[u]
            return carry

        lax.fori_loop(0, tq // qr // unroll, block, 0)

    chunk = 128

    def merge(i, carry):
        rs = pl.ds(pl.multiple_of(i * chunk, chunk), chunk)
        for h in range(A_HEADS_PER_GROUP):
            ls = [lse_ref[g, h, rs, :] for g in range(3)]
            mx = jnp.maximum(jnp.maximum(ls[0], ls[1]), ls[2])
            num = jnp.zeros((chunk, HEAD_DIM), F32)
            dn = jnp.zeros((chunk, HEAD_DIM), F32)
            for g in range(3):
                e = jnp.exp(ls[g] - mx)
                num = num + e * og_ref[g, h, rs, :]
                dn = dn + e
            o_ref[rs, h * HEAD_DIM:(h + 1) * HEAD_DIM] = (num / dn).astype(o_ref.dtype)
        return carry

    lax.fori_loop(0, tq // chunk, merge, 0)


def _attn_a(p_nat, a4, a16, bias, batch, seq, tq=1024):
    m = p_nat.shape[0]
    nt = seq // tq
    blk = A_BLOCK
    gw = A_GROUP_W
    in_specs = []
    args = []

    nb_tile = tq // blk
    last64 = m // blk - 1
    q_col, k_col, v_col = NAT_Q0 // gw, NAT_K0 // gw, NAT_V0 // gw

    def g0_main(col):
        return pl.BlockSpec((tq, gw), lambda b, t: (b * nt + t, col))

    def g0_prev(col):
        return pl.BlockSpec((blk, gw), lambda b, t: (jnp.maximum((b * nt + t) * nb_tile - 1, 0), col))

    def g0_next(col):
        return pl.BlockSpec((blk, gw), lambda b, t: (jnp.minimum((b * nt + t + 1) * nb_tile, last64), col))

    in_specs += [g0_main(q_col), g0_prev(k_col), g0_main(k_col), g0_next(k_col),
                 g0_prev(v_col), g0_main(v_col), g0_next(v_col)]
    args += [p_nat] * 7

    for arr, d in ((a4, 4), (a16, 16)):
        n = tq // d
        nb = n // blk
        lastb = seq // d // blk - 1

        def main(col, d=d, n=n):
            return pl.BlockSpec((1, d, n, gw), lambda b, t: (b, 0, t, col))

        def prev(col, d=d, nb=nb):
            return pl.BlockSpec((1, d, blk, gw), lambda b, t: (b, 0, jnp.maximum(t * nb - 1, 0), col))

        def nxt(col, d=d, nb=nb, lastb=lastb):
            return pl.BlockSpec((1, d, blk, gw), lambda b, t: (b, 0, jnp.minimum((t + 1) * nb, lastb), col))

        in_specs += [main(0), prev(1), main(1), nxt(1), prev(2), main(2), nxt(2)]
        args += [arr] * 7

    for tab in bias:
        in_specs.append(pl.BlockSpec(tab.shape, lambda b, t: (0, 0, 0, 0), pipeline_mode=pl.Buffered(1)))
        args.append(tab)

    return pl.pallas_call(
        functools.partial(_attn_a_kernel, tq=tq),
        grid=(batch, nt),
        in_specs=in_specs,
        out_specs=pl.BlockSpec((tq, gw), lambda b, t: (b * nt + t, 0)),
        out_shape=jax.ShapeDtypeStruct((m, gw), BF16),
        scratch_shapes=(
            [pltpu.VMEM((3, A_HEADS_PER_GROUP, tq, HEAD_DIM), F32)] * 2
            + [pltpu.VMEM((d, tq // d + 2 * blk, gw), BF16) for _, d in A_GROUPS[:2] for _ in range(2)]),
        compiler_params=_params(("arbitrary", "arbitrary")),
        name="attn_a",
    )(*args)


def _attn_b_kernel(q_ref, kp_ref, k_ref, kn_ref, vp_ref, v_ref, vn_ref, bias_ref, sink_ref, o_ref,
                   kw_ref, vw_ref, *, tq):
    t = pl.program_id(1)
    nt = pl.num_programs(1)
    blk = B_BLOCK
    nblk = tq // blk

    kw_ref[0:blk, :] = kp_ref[...]
    kw_ref[blk:blk + tq, :] = k_ref[...]
    kw_ref[blk + tq:, :] = kn_ref[...]
    vw_ref[:, 0:blk] = vp_ref[...]
    vw_ref[:, blk:blk + tq] = v_ref[...]
    vw_ref[:, blk + tq:] = vn_ref[...]
    ones = jnp.ones((B_ONES_ROWS, 3 * blk), BF16)

    def block(it, carry):
        units = []
        for j in range(B_UNROLL):
            c = it * B_UNROLL + j
            variant = _edge_variant(c, nblk, t, nt)
            rows = pl.ds(pl.multiple_of(c * blk, blk), blk)
            wrows = pl.ds(pl.multiple_of(c * blk, blk), 3 * blk)
            for kvh in range(B_KV_HEADS):
                units.append((kvh, variant, rows, wrows))
        scores = []
        for kvh, variant, rows, wrows in units:
            kcols = slice(kvh * HEAD_DIM, (kvh + 1) * HEAD_DIM)
            q4 = jnp.concatenate(
                [q_ref[rows, (kvh * B_REP + rep) * HEAD_DIM:(kvh * B_REP + rep + 1) * HEAD_DIM]
                 for rep in range(B_REP)], axis=0)
            scores.append(lax.dot_general(kw_ref[wrows, kcols], q4, (((1,), (1,)), ((), ())),
                                          preferred_element_type=F32))
        probs = []
        for (kvh, variant, rows, wrows), s in zip(units, scores):
            s = s + bias_ref[variant, kvh]
            sink = sink_ref[kvh]
            m = jnp.maximum(jnp.max(s, axis=0, keepdims=True), sink)
            probs.append((jnp.exp(s - m).astype(BF16), jnp.exp(sink - m)))
        for (kvh, variant, rows, wrows), (p, psink) in zip(units, probs):
            kcols = slice(kvh * HEAD_DIM, (kvh + 1) * HEAD_DIM)
            vx = jnp.concatenate([vw_ref[kcols, wrows], ones], axis=0)
            acc = jnp.dot(vx, p, preferred_element_type=F32)
            den = acc[HEAD_DIM:HEAD_DIM + 1, :] + psink
            o = (acc[:HEAD_DIM, :] / den).astype(o_ref.dtype)
            for rep in range(B_REP):
                h = kvh * B_REP + rep
                o_ref[h * HEAD_DIM:(h + 1) * HEAD_DIM, rows] = o[:, rep * blk:(rep + 1) * blk]
        return carry

    lax.fori_loop(0, nblk // B_UNROLL, block, 0)


def _attn_b(p_nat, vbt, bias, sink, batch, seq, tq=1024):
    m = p_nat.shape[0]
    nt = seq // tq
    blk = B_BLOCK
    nb_tile = tq // blk
    lastb = m // blk - 1
    q_col = NAT_QB // B_Q_W
    k_col = NAT_KB // B_KV_W

    def first(b, t):
        return jnp.maximum((b * nt + t) * nb_tile - 1, 0)

    def after(b, t):
        return jnp.minimum((b * nt + t + 1) * nb_tile, lastb)

    return pl.pallas_call(
        functools.partial(_attn_b_kernel, tq=tq),
        grid=(batch, nt),
        in_specs=[
            pl.BlockSpec((tq, B_Q_W), lambda b, t: (b * nt + t, q_col)),
            pl.BlockSpec((blk, B_KV_W), lambda b, t: (first(b, t), k_col)),
            pl.BlockSpec((tq, B_KV_W), lambda b, t: (b * nt + t, k_col)),
            pl.BlockSpec((blk, B_KV_W), lambda b, t: (after(b, t), k_col)),
            pl.BlockSpec((B_KV_W, blk), lambda b, t: (0, first(b, t))),
            pl.BlockSpec((B_KV_W, tq), lambda b, t: (0, b * nt + t)),
            pl.BlockSpec((B_KV_W, blk), lambda b, t: (0, after(b, t))),
            pl.BlockSpec(bias.shape, lambda b, t: (0, 0, 0, 0)),
            pl.BlockSpec(sink.shape, lambda b, t: (0, 0, 0)),
        ],
        out_specs=pl.BlockSpec((B_Q_W, tq), lambda b, t: (0, b * nt + t)),
        out_shape=jax.ShapeDtypeStruct((B_Q_W, m), BF16),
        scratch_shapes=[pltpu.VMEM((tq + 2 * blk, B_KV_W), BF16), pltpu.VMEM((B_KV_W, tq + 2 * blk), BF16)],
        compiler_params=_params(("arbitrary", "arbitrary")),
        name="attn_b",
    )(p_nat, p_nat, p_nat, p_nat, vbt, vbt, vbt, bias, sink)


def _mix_ln_kernel(x_ref, oa_ref, ob_ref, ga_ref, gb_ref, wpa_ref, wpb_ref, wo_ref, g_ref, b_ref, o_ref, *, alpha):
    sub = x_ref.shape[0] // MIX_SUBTILES
    ys = []
    for k in range(MIX_SUBTILES):
        rows = slice(k * sub, (k + 1) * sub)
        ya = jnp.dot(oa_ref[rows, :], wpa_ref[...], preferred_element_type=F32)
        yb = lax.dot_general(ob_ref[:, rows], wpb_ref[...], (((0,), (0,)), ((), ())),
                             preferred_element_type=F32)
        ys.append((ya, yb))
    for k, (ya, yb) in enumerate(ys):
        rows = slice(k * sub, (k + 1) * sub)
        merged = (jax.nn.sigmoid(ga_ref[rows, :].astype(F32)) * ya
                  + jax.nn.sigmoid(gb_ref[rows, :].astype(F32)) * yb)
        z = jnp.dot(merged.astype(BF16), wo_ref[...], preferred_element_type=F32)
        o_ref[rows, :] = _layer_norm(alpha * x_ref[rows, :] + z, g_ref[...], b_ref[...])


def _mix_ln(x, oa, ob, p_nat, w_pa, w_pb, w_out, ln_g, ln_b, alpha, tm=512):
    m, dm = x.shape

    def const(arr):
        return pl.BlockSpec(arr.shape, lambda i: (0, 0), pipeline_mode=pl.Buffered(1))

    return pl.pallas_call(
        functools.partial(_mix_ln_kernel, alpha=alpha),
        grid=(m // tm,),
        in_specs=[
            pl.BlockSpec((tm, dm), lambda i: (i, 0)),
            pl.BlockSpec((tm, oa.shape[1]), lambda i: (i, 0)),
            pl.BlockSpec((ob.shape[0], tm), lambda i: (0, i)),
            pl.BlockSpec((tm, dm), lambda i: (i, NAT_GA // dm)),
            pl.BlockSpec((tm, dm), lambda i: (i, NAT_GB // dm)),
            const(w_pa), const(w_pb), const(w_out), const(ln_g), const(ln_b),
        ],
        out_specs=pl.BlockSpec((tm, dm), lambda i: (i, 0)),
        out_shape=jax.ShapeDtypeStruct((m, dm), F32),
        compiler_params=_params(("arbitrary",)),
        name="mix_ln",
    )(x, oa, ob, p_nat, p_nat, w_pa, w_pb, w_out, ln_g, ln_b)


def _ffn_step(x_ref, wg_ref, wu_ref, wd_ref, g_ref, b_ref, o_ref, alpha, first, last):
    sub = x_ref.shape[0] // FFN_SUBTILES
    hs = []
    for k in range(FFN_SUBTILES):
        xb = x_ref[k * sub:(k + 1) * sub, :].astype(BF16)
        hs.append((jnp.dot(xb, wg_ref[...], preferred_element_type=F32),
                   jnp.dot(xb, wu_ref[...], preferred_element_type=F32)))
    for k, (gate, up) in enumerate(hs):
        rows = slice(k * sub, (k + 1) * sub)
        act = (gate * jax.nn.sigmoid(gate) * up).astype(BF16)
        base = alpha * x_ref[rows, :] if first else o_ref[rows, :]
        y = base + jnp.dot(act, wd_ref[...], preferred_element_type=F32)
        o_ref[rows, :] = _layer_norm(y, g_ref[...], b_ref[...]) if last else y


def _ffn_ln_kernel(x_ref, wg_ref, wu_ref, wd_ref, g_ref, b_ref, o_ref, *, alpha):
    f = pl.program_id(1)
    nf = pl.num_programs(1)
    step = functools.partial(_ffn_step, x_ref, wg_ref, wu_ref, wd_ref, g_ref, b_ref, o_ref, alpha)
    pl.when(f == 0)(functools.partial(step, True, False))
    pl.when(jnp.logical_and(f > 0, f < nf - 1))(functools.partial(step, False, False))
    pl.when(f == nf - 1)(functools.partial(step, False, True))


def _ffn_ln(x, w_up, w_down, ln_g, ln_b, alpha, tf=512, tm=1024):
    m, dm = x.shape
    nf = w_down.shape[0] // tf
    return pl.pallas_call(
        functools.partial(_ffn_ln_kernel, alpha=alpha),
        grid=(m // tm, nf),
        in_specs=[
            pl.BlockSpec((tm, dm), lambda i, f: (i, 0)),
            pl.BlockSpec((dm, tf), lambda i, f: (0, f)),
            pl.BlockSpec((dm, tf), lambda i, f: (0, nf + f)),
            pl.BlockSpec((tf, dm), lambda i, f: (f, 0)),
            pl.BlockSpec((1, dm), lambda i, f: (0, 0)),
            pl.BlockSpec((1, dm), lambda i, f: (0, 0)),
        ],
        out_specs=pl.BlockSpec((tm, dm), lambda i, f: (i, 0)),
        out_shape=jax.ShapeDtypeStruct((m, dm), F32),
        compiler_params=_params(("arbitrary", "arbitrary")),
        name="ffn_ln",
    )(x, w_up, w_up, w_down, ln_g, ln_b)


def _bias_tables(rel_bias):
    def toeplitz(rows, halo, dil, cols):
        width = rows + 2 * halo
        deltas = np.arange(-(halo + rows - 1), rows + halo)
        diag = rel_bias[_t5_bucket(deltas * dil)][:, cols].T.astype(F32)
        return jnp.stack([diag[:, rows - 1 - i:rows - 1 - i + width] for i in range(rows)], axis=1)

    bias_a = []
    for g, (window, dil) in enumerate(A_GROUPS):
        rows = A_QROWS[g]
        heads = toeplitz(rows, A_BLOCK, dil, slice(g * A_HEADS_PER_GROUP, (g + 1) * A_HEADS_PER_GROUP))
        masks = _band_masks(rows, A_BLOCK, window // (2 * dil))
        bias_a.append(jnp.where(masks[:, None], heads[None], NEG_INF))

    heads = toeplitz(B_BLOCK, B_BLOCK, 1, slice(A_HEADS, A_HEADS + B_Q_HEADS))
    masks = _band_masks(B_BLOCK, B_BLOCK, B_HALF)
    bias_b = jnp.where(masks[:, None], heads[None], NEG_INF)
    bias_b = bias_b.reshape(3, B_KV_HEADS, B_REP, B_BLOCK, 3 * B_BLOCK).transpose(0, 1, 4, 2, 3)
    bias_b = bias_b.reshape(3, B_KV_HEADS, 3 * B_BLOCK, B_REP * B_BLOCK)
    return bias_a, bias_b


def _prep_w_in(w_in):
    qa, ka, va = (w_in[:, i * A_QKV_W:(i + 1) * A_QKV_W] for i in range(3))
    o = 3 * A_QKV_W
    qb = w_in[:, o:o + B_Q_W]
    kb = w_in[:, o + B_Q_W:o + B_Q_W + B_KV_W]
    vb = w_in[:, o + B_Q_W + B_KV_W:o + B_Q_W + 2 * B_KV_W]
    o += B_Q_W + 2 * B_KV_W
    dm = w_in.shape[0]
    ga = w_in[:, o:o + dm]
    gb = w_in[:, o + dm:o + 2 * dm]

    def grp(w, g):
        return w[:, g * A_GROUP_W:(g + 1) * A_GROUP_W]

    w_nat = jnp.concatenate([ga, gb, qb, grp(qa, 0), grp(ka, 0), grp(va, 0), kb, vb], axis=1).astype(BF16)
    w_perm = [jnp.concatenate([grp(qa, g), grp(ka, g), grp(va, g)], axis=1).astype(BF16) for g in (1, 2)]
    s_nat = np.ones((1, NAT_W), np.float32)
    s_nat[:, NAT_QB:NAT_QB + B_Q_W] = QK_SCALE
    s_nat[:, NAT_Q0:NAT_Q0 + A_GROUP_W] = QK_SCALE
    return w_nat, w_perm, jnp.asarray(s_nat)


def kernel(x, rel_bias, w_in, sink, w_pa, w_pb, w_out, ln1_g, ln1_b, w_up, w_down, ln2_g, ln2_b):
    batch, seq, dm = x.shape
    depth = w_in.shape[0]
    alpha = (2 * depth) ** 0.25
    bias_a, bias_b = _bias_tables(rel_bias)
    h = x.reshape(batch * seq, dm)
    for l in range(depth):
        w_nat, (w4, w16), s_nat = _prep_w_in(w_in[l])
        sink_row = jnp.repeat(sink[l].reshape(B_KV_HEADS, B_REP), B_BLOCK, axis=1)[:, None, :].astype(F32)
        p_nat, vbt, hb = _proj_nat(h, w_nat, s_nat)
        a4, a16 = _proj_perm(hb, w4, w16, batch, seq)
        oa = _attn_a(p_nat, a4, a16, bias_a, batch, seq)
        ob = _attn_b(p_nat, vbt, bias_b, sink_row, batch, seq)
        h = _mix_ln(h, oa, ob, p_nat, w_pa[l].astype(BF16), w_pb[l].astype(BF16), w_out[l].astype(BF16),
                    ln1_g[l][None], ln1_b[l][None], alpha)
        h = _ffn_ln(h, w_up[l].astype(BF16), w_down[l].astype(BF16), ln2_g[l][None], ln2_b[l][None], alpha)
    return h.reshape(batch, seq, dm)
```

```python
import functools
import math

import numpy as np
import jax
import jax.numpy as jnp
from jax import lax
from jax.experimental import pallas as pl
from jax.experimental.pallas import tpu as pltpu

HEAD_DIM = 128
A_GROUPS = ((128, 1), (512, 4), (2048, 16))
A_HEADS_PER_GROUP = 4
A_HEADS = A_HEADS_PER_GROUP * len(A_GROUPS)
A_HALO = 64
A_GROUP_W = A_HEADS_PER_GROUP * HEAD_DIM
A_QKV_W = A_HEADS * HEAD_DIM
B_Q_HEADS = 8
B_KV_HEADS = 2
B_REP = B_Q_HEADS // B_KV_HEADS
B_HALF = 128
B_BLOCK = 128
B_Q_W = B_Q_HEADS * HEAD_DIM
B_KV_W = B_KV_HEADS * HEAD_DIM
NUM_BUCKETS = 32
MAX_DISTANCE = 1024
LN_EPS = 1e-5
NEG_INF = -1e30
QK_SCALE = HEAD_DIM ** -0.5
A_QROWS = 128
B_UNROLL = 4
B_ONES_ROWS = 16
MIX_SUBTILES = 4
FFN_SUBTILES = 4

BF16 = jnp.bfloat16
F32 = jnp.float32

NAT_GA, NAT_GB, NAT_QB, NAT_Q0, NAT_K0, NAT_V0, NAT_KB, NAT_VB = 0, 2048, 4096, 5120, 5632, 6144, 6656, 6912
NAT_W = 7168

VMEM_LIMIT = 56 * 1024 * 1024


def _t5_bucket(rel):
    half = NUM_BUCKETS // 2
    max_exact = half // 2
    n = np.abs(rel)
    scaled = np.log(np.maximum(n, 1) / max_exact) / math.log(MAX_DISTANCE / max_exact)
    large = np.minimum(max_exact + (scaled * (half - max_exact)).astype(np.int64), half - 1)
    return (np.where(rel > 0, half, 0) + np.where(n < max_exact, n, large)).astype(np.int32)


def _band_masks(rows, halo, half):
    kj = np.arange(rows + 2 * halo)[None, :]
    band = np.abs(kj - halo - np.arange(rows)[:, None]) <= half
    return np.stack([band, band & (kj >= halo), band & (kj < rows + halo)])


def _layer_norm(y, g, b):
    mu = jnp.mean(y, axis=-1, keepdims=True)
    yc = y - mu
    var = jnp.mean(yc * yc, axis=-1, keepdims=True)
    return yc * lax.rsqrt(var + LN_EPS) * g + b


def _params(sem):
    return pltpu.CompilerParams(dimension_semantics=sem, vmem_limit_bytes=VMEM_LIMIT)


def _proj_nat_kernel(x_ref, w_ref, s_ref, o_ref, vt_ref, xb_ref):
    j = pl.program_id(1)

    @pl.when(j == 0)
    def _():
        xb_ref[...] = x_ref[...].astype(BF16)

    res = jnp.dot(xb_ref[...], w_ref[...], preferred_element_type=F32) * s_ref[...]
    o_ref[...] = res.astype(o_ref.dtype)

    @pl.when(j == pl.num_programs(1) - 1)
    def _():
        vt_ref[...] = res[:, res.shape[1] - B_KV_W:].T.astype(vt_ref.dtype)


def _proj_nat(x, w, colscale, tm=1024, tn=1792):
    m, k = x.shape
    n = w.shape[1]
    return pl.pallas_call(
        _proj_nat_kernel,
        grid=(m // tm, n // tn),
        in_specs=[
            pl.BlockSpec((tm, k), lambda i, j: (i, 0)),
            pl.BlockSpec((k, tn), lambda i, j: (0, j)),
            pl.BlockSpec((1, tn), lambda i, j: (0, j)),
        ],
        out_specs=[pl.BlockSpec((tm, tn), lambda i, j: (i, j)),
                   pl.BlockSpec((B_KV_W, tm), lambda i, j: (0, i)),
                   pl.BlockSpec((tm, k), lambda i, j: (i, 0))],
        out_shape=[jax.ShapeDtypeStruct((m, n), BF16), jax.ShapeDtypeStruct((B_KV_W, m), BF16),
                   jax.ShapeDtypeStruct((m, k), BF16)],
        compiler_params=_params(("arbitrary", "arbitrary")),
        name="proj_nat",
    )(x, w, colscale)


def _proj_perm_kernel(xb_ref, w4_ref, w16_ref, o4_ref, o16_ref, res_ref, tmp_ref, *, tm):
    gw = A_GROUP_W
    ncb = gw // HEAD_DIM
    q4 = tm // 4
    for g, (d, w_ref, o_ref) in enumerate(((4, w4_ref, o4_ref), (16, w16_ref, o16_ref))):
        for part in range(3):
            slot = 3 * g + part
            cols = slice(part * gw, (part + 1) * gw)
            res = jnp.dot(xb_ref[...], w_ref[:, cols], preferred_element_type=F32)
            if part == 0:
                res = res * QK_SCALE
            for cb in range(ncb):
                res_ref[slot, cb] = res[:, cb * HEAD_DIM:(cb + 1) * HEAD_DIM]
            for cb in range(ncb):
                ocols = slice(part * gw + cb * HEAD_DIM, part * gw + (cb + 1) * HEAD_DIM)
                if d == 4:
                    for r in range(4):
                        o_ref[0, r, :, ocols] = res_ref[slot, cb, pl.ds(r, q4, stride=4), :].astype(BF16)
                else:
                    for c1 in range(4):
                        tmp_ref[part, cb, c1 * q4:(c1 + 1) * q4, :] = res_ref[slot, cb, pl.ds(c1, q4, stride=4), :]
                    for c1 in range(4):
                        for c2 in range(4):
                            o_ref[0, c1 + 4 * c2, :, ocols] = (
                                tmp_ref[part, cb, pl.ds(c1 * q4 + c2, q4 // 4, stride=4), :].astype(BF16))


def _proj_perm(xb, w4, w16, batch, seq, tm=1024):
    m, k = xb.shape
    n = w4.shape[1]
    nt = seq // tm
    outs = []
    out_specs = []
    for d in (4, 16):
        outs.append(jax.ShapeDtypeStruct((batch, d, seq // d, n), BF16))
        out_specs.append(pl.BlockSpec((1, d, tm // d, n), lambda i: (i // nt, 0, i % nt, 0)))
    return pl.pallas_call(
        functools.partial(_proj_perm_kernel, tm=tm),
        grid=(m // tm,),
        in_specs=[
            pl.BlockSpec((tm, k), lambda i: (i, 0)),
            pl.BlockSpec((k, n), lambda i: (0, 0), pipeline_mode=pl.Buffered(1)),
            pl.BlockSpec((k, n), lambda i: (0, 0), pipeline_mode=pl.Buffered(1)),
        ],
        out_specs=out_specs,
        out_shape=outs,
        scratch_shapes=[pltpu.VMEM((6, A_GROUP_W // HEAD_DIM, tm, HEAD_DIM), F32),
                        pltpu.VMEM((3, A_GROUP_W // HEAD_DIM, tm, HEAD_DIM), F32)],
        compiler_params=_params(("arbitrary",)),
        name="proj_perm",
    )(xb, w4, w16)


def _edge_variant(c, nblk, t, nt):
    first = jnp.logical_and(c == 0, t == 0)
    last = jnp.logical_and(c == nblk - 1, t == nt - 1)
    return jnp.where(first, 1, jnp.where(last, 2, 0))


def _attn_a_kernel(*refs, tq):
    ins = refs[:21]
    bias_refs = refs[21:24]
    o_ref = refs[24]
    og_ref, lse_ref = refs[25:27]
    win_refs = refs[27:33]
    t = pl.program_id(1)
    nt = pl.num_programs(1)
    h = pl.program_id(2)
    halo = A_HALO
    qr = A_QROWS

    for g, (_, d) in enumerate(A_GROUPS):
        q_ref, kp_ref, k_ref, kn_ref, vp_ref, v_ref, vn_ref = ins[7 * g:7 * g + 7]
        kw_ref, vw_ref = win_refs[2 * g:2 * g + 2]
        bias_ref = bias_refs[g]
        n = tq // d
        nblk = n // qr

        for w_ref, p_ref, c_ref, n_ref in ((kw_ref, kp_ref, k_ref, kn_ref), (vw_ref, vp_ref, v_ref, vn_ref)):
            w_ref[:, 0:halo, :] = p_ref[...].reshape(d, halo, HEAD_DIM)
            w_ref[:, halo:halo + n, :] = c_ref[...].reshape(d, n, HEAD_DIM)
            w_ref[:, halo + n:, :] = n_ref[...].reshape(d, halo, HEAD_DIM)

        qs, kws, vws, biases, dsts = [], [], [], [], []
        for r in range(d):
            for c in range(nblk):
                qrows = slice(c * qr, (c + 1) * qr)
                wrows = slice(c * qr, (c + 1) * qr + 2 * halo)
                qs.append(q_ref[qrows, :] if g == 0 else q_ref[0, r, qrows, :])
                kws.append(kw_ref[r, wrows, :])
                vws.append(vw_ref[r, wrows, :])
                biases.append(bias_ref[_edge_variant(c, nblk, t, nt), h])
                dsts.append(qrows if d == 1 else pl.ds(r + d * qr * c, qr, stride=d))
        q3 = jnp.stack(qs)
        kw3 = jnp.stack(kws)
        vw3 = jnp.stack(vws)
        s = jnp.einsum("uqd,ukd->uqk", q3, kw3, preferred_element_type=F32) + jnp.stack(biases)
        m = jnp.max(s, axis=-1, keepdims=True)
        p = jnp.exp(s - m).astype(BF16)
        vx = jnp.concatenate([vw3, jnp.ones_like(vw3)], axis=-1)
        acc = jnp.einsum("uqk,ukd->uqd", p, vx, preferred_element_type=F32)
        den = acc[..., HEAD_DIM:]
        og = acc[..., :HEAD_DIM] / den
        lse = m + jnp.log(den)
        for u, dst in enumerate(dsts):
            og_ref[g, dst, :] = og[u]
            lse_ref[g, dst, :] = lse[u]

    chunk = 256

    def merge(i, carry):
        rs = pl.ds(pl.multiple_of(i * chunk, chunk), chunk)
        ls = [lse_ref[g, rs, :] for g in range(3)]
        mx = jnp.maximum(jnp.maximum(ls[0], ls[1]), ls[2])
        num = jnp.zeros((chunk, HEAD_DIM), F32)
        dn = jnp.zeros((chunk, HEAD_DIM), F32)
        for g in range(3):
            e = jnp.exp(ls[g] - mx)
            num = num + e * og_ref[g, rs, :]
            dn = dn + e
        o_ref[rs, :] = (num / dn).astype(o_ref.dtype)
        return carry

    lax.fori_loop(0, tq // chunk, merge, 0)


def _attn_a(p_nat, a4, a16, bias, batch, seq, tq=2048):
    m = p_nat.shape[0]
    nt = seq // tq
    halo = A_HALO
    hd = HEAD_DIM
    hg = A_HEADS_PER_GROUP
    in_specs = []
    args = []

    nb_tile = tq // halo
    last_halo = m // halo - 1

    def g0_main(col0):
        return pl.BlockSpec((tq, hd), lambda b, t, h: (b * nt + t, col0 + h))

    def g0_prev(col0):
        return pl.BlockSpec((halo, hd), lambda b, t, h: (jnp.maximum((b * nt + t) * nb_tile - 1, 0), col0 + h))

    def g0_next(col0):
        return pl.BlockSpec((halo, hd),
                            lambda b, t, h: (jnp.minimum((b * nt + t + 1) * nb_tile, last_halo), col0 + h))

    q0, k0, v0 = NAT_Q0 // hd, NAT_K0 // hd, NAT_V0 // hd
    in_specs += [g0_main(q0), g0_prev(k0), g0_main(k0), g0_next(k0), g0_prev(v0), g0_main(v0), g0_next(v0)]
    args += [p_nat] * 7

    for arr, d in ((a4, 4), (a16, 16)):
        n = tq // d
        nb = n // halo
        lastb = seq // d // halo - 1

        def main(part, d=d, n=n):
            return pl.BlockSpec((1, d, n, hd), lambda b, t, h: (b, 0, t, part * hg + h))

        def prev(part, d=d, nb=nb):
            return pl.BlockSpec((1, d, halo, hd),
                                lambda b, t, h: (b, 0, jnp.maximum(t * nb - 1, 0), part * hg + h))

        def nxt(part, d=d, nb=nb, lastb=lastb):
            return pl.BlockSpec((1, d, halo, hd),
                                lambda b, t, h: (b, 0, jnp.minimum((t + 1) * nb, lastb), part * hg + h))

        in_specs += [main(0), prev(1), main(1), nxt(1), prev(2), main(2), nxt(2)]
        args += [arr] * 7

    for tab in bias:
        in_specs.append(pl.BlockSpec(tab.shape, lambda b, t, h: (0, 0, 0, 0), pipeline_mode=pl.Buffered(1)))
        args.append(tab)

    return pl.pallas_call(
        functools.partial(_attn_a_kernel, tq=tq),
        grid=(batch, nt, hg),
        in_specs=in_specs,
        out_specs=pl.BlockSpec((tq, hd), lambda b, t, h: (b * nt + t, h)),
        out_shape=jax.ShapeDtypeStruct((m, A_GROUP_W), BF16),
        scratch_shapes=(
            [pltpu.VMEM((3, tq, hd), F32)] * 2
            + [pltpu.VMEM((d, tq // d + 2 * halo, hd), BF16) for _, d in A_GROUPS for _ in range(2)]),
        compiler_params=_params(("arbitrary", "arbitrary", "arbitrary")),
        name="attn_a",
    )(*args)


def _attn_b_kernel(q_ref, kp_ref, k_ref, kn_ref, vp_ref, v_ref, vn_ref, bias_ref, sink_ref, o_ref,
                   kw_ref, vw_ref, *, tq):
    t = pl.program_id(1)
    nt = pl.num_programs(1)
    blk = B_BLOCK
    nblk = tq // blk

    kw_ref[0:blk, :] = kp_ref[...]
    kw_ref[blk:blk + tq, :] = k_ref[...]
    kw_ref[blk + tq:, :] = kn_ref[...]
    vw_ref[:, 0:blk] = vp_ref[...]
    vw_ref[:, blk:blk + tq] = v_ref[...]
    vw_ref[:, blk + tq:] = vn_ref[...]
    ones = jnp.ones((B_ONES_ROWS, 3 * blk), BF16)

    def block(it, carry):
        units = []
        for j in range(B_UNROLL):
            c = it * B_UNROLL + j
            variant = _edge_variant(c, nblk, t, nt)
            rows = pl.ds(pl.multiple_of(c * blk, blk), blk)
            wrows = pl.ds(pl.multiple_of(c * blk, blk), 3 * blk)
            for kvh in range(B_KV_HEADS):
                units.append((kvh, variant, rows, wrows))
        scores = []
        for kvh, variant, rows, wrows in units:
            kcols = slice(kvh * HEAD_DIM, (kvh + 1) * HEAD_DIM)
            q4 = jnp.concatenate(
                [q_ref[rows, (kvh * B_REP + rep) * HEAD_DIM:(kvh * B_REP + rep + 1) * HEAD_DIM]
                 for rep in range(B_REP)], axis=0)
            scores.append(lax.dot_general(kw_ref[wrows, kcols], q4, (((1,), (1,)), ((), ())),
                                          preferred_element_type=F32))
        probs = []
        for (kvh, variant, rows, wrows), s in zip(units, scores):
            s = s + bias_ref[variant, kvh]
            sink = sink_ref[kvh]
            m = jnp.maximum(jnp.max(s, axis=0, keepdims=True), sink)
            probs.append((jnp.exp(s - m).astype(BF16), jnp.exp(sink - m)))
        for (kvh, variant, rows, wrows), (p, psink) in zip(units, probs):
            kcols = slice(kvh * HEAD_DIM, (kvh + 1) * HEAD_DIM)
            vx = jnp.concatenate([vw_ref[kcols, wrows], ones], axis=0)
            acc = jnp.dot(vx, p, preferred_element_type=F32)
            den = acc[HEAD_DIM:HEAD_DIM + 1, :] + psink
            o = (acc[:HEAD_DIM, :] / den).astype(o_ref.dtype)
            for rep in range(B_REP):
                h = kvh * B_REP + rep
                o_ref[h * HEAD_DIM:(h + 1) * HEAD_DIM, rows] = o[:, rep * blk:(rep + 1) * blk]
        return carry

    lax.fori_loop(0, nblk // B_UNROLL, block, 0)


def _attn_b(p_nat, vbt, bias, sink, batch, seq, tq=1024):
    m = p_nat.shape[0]
    nt = seq // tq
    blk = B_BLOCK
    nb_tile = tq // blk
    lastb = m // blk - 1
    q_col = NAT_QB // B_Q_W
    k_col = NAT_KB // B_KV_W

    def first(b, t):
        return jnp.maximum((b * nt + t) * nb_tile - 1, 0)

    def after(b, t):
        return jnp.minimum((b * nt + t + 1) * nb_tile, lastb)

    return pl.pallas_call(
        functools.partial(_attn_b_kernel, tq=tq),
        grid=(batch, nt),
        in_specs=[
            pl.BlockSpec((tq, B_Q_W), lambda b, t: (b * nt + t, q_col)),
            pl.BlockSpec((blk, B_KV_W), lambda b, t: (first(b, t), k_col)),
            pl.BlockSpec((tq, B_KV_W), lambda b, t: (b * nt + t, k_col)),
            pl.BlockSpec((blk, B_KV_W), lambda b, t: (after(b, t), k_col)),
            pl.BlockSpec((B_KV_W, blk), lambda b, t: (0, first(b, t))),
            pl.BlockSpec((B_KV_W, tq), lambda b, t: (0, b * nt + t)),
            pl.BlockSpec((B_KV_W, blk), lambda b, t: (0, after(b, t))),
            pl.BlockSpec(bias.shape, lambda b, t: (0, 0, 0, 0)),
            pl.BlockSpec(sink.shape, lambda b, t: (0, 0, 0)),
        ],
        out_specs=pl.BlockSpec((B_Q_W, tq), lambda b, t: (0, b * nt + t)),
        out_shape=jax.ShapeDtypeStruct((B_Q_W, m), BF16),
        scratch_shapes=[pltpu.VMEM((tq + 2 * blk, B_KV_W), BF16), pltpu.VMEM((B_KV_W, tq + 2 * blk), BF16)],
        compiler_params=_params(("arbitrary", "arbitrary")),
        name="attn_b",
    )(p_nat, p_nat, p_nat, p_nat, vbt, vbt, vbt, bias, sink)


def _mix_ln_kernel(x_ref, oa_ref, ob_ref, ga_ref, gb_ref, wpa_ref, wpb_ref, wo_ref, g_ref, b_ref, o_ref, *, alpha):
    sub = x_ref.shape[0] // MIX_SUBTILES
    ys = []
    for k in range(MIX_SUBTILES):
        rows = slice(k * sub, (k + 1) * sub)
        ya = jnp.dot(oa_ref[rows, :], wpa_ref[...], preferred_element_type=F32)
        yb = lax.dot_general(ob_ref[:, rows], wpb_ref[...], (((0,), (0,)), ((), ())),
                             preferred_element_type=F32)
        ys.append((ya, yb))
    for k, (ya, yb) in enumerate(ys):
        rows = slice(k * sub, (k + 1) * sub)
        merged = (jax.nn.sigmoid(ga_ref[rows, :].astype(F32)) * ya
                  + jax.nn.sigmoid(gb_ref[rows, :].astype(F32)) * yb)
        z = jnp.dot(merged.astype(BF16), wo_ref[...], preferred_element_type=F32)
        o_ref[rows, :] = _layer_norm(alpha * x_ref[rows, :] + z, g_ref[...], b_ref[...])


def _mix_ln(x, oa, ob, p_nat, w_pa, w_pb, w_out, ln_g, ln_b, alpha, tm=512):
    m, dm = x.shape

    def const(arr):
        return pl.BlockSpec(arr.shape, lambda i: (0, 0), pipeline_mode=pl.Buffered(1))

    return pl.pallas_call(
        functools.partial(_mix_ln_kernel, alpha=alpha),
        grid=(m // tm,),
        in_specs=[
            pl.BlockSpec((tm, dm), lambda i: (i, 0)),
            pl.BlockSpec((tm, oa.shape[1]), lambda i: (i, 0)),
            pl.BlockSpec((ob.shape[0], tm), lambda i: (0, i)),
            pl.BlockSpec((tm, dm), lambda i: (i, NAT_GA // dm)),
            pl.BlockSpec((tm, dm), lambda i: (i, NAT_GB // dm)),
            const(w_pa), const(w_pb), const(w_out), const(ln_g), const(ln_b),
        ],
        out_specs=pl.BlockSpec((tm, dm), lambda i: (i, 0)),
        out_shape=jax.ShapeDtypeStruct((m, dm), F32),
        compiler_params=_params(("arbitrary",)),
        name="mix_ln",
    )(x, oa, ob, p_nat, p_nat, w_pa, w_pb, w_out, ln_g, ln_b)


def _ffn_step(x_ref, wg_ref, wu_ref, wd_ref, g_ref, b_ref, o_ref, alpha, first, last):
    sub = x_ref.shape[0] // FFN_SUBTILES
    hs = []
    for k in range(FFN_SUBTILES):
        xb = x_ref[k * sub:(k + 1) * sub, :].astype(BF16)
        hs.append((jnp.dot(xb, wg_ref[...], preferred_element_type=F32),
                   jnp.dot(xb, wu_ref[...], preferred_element_type=F32)))
    for k, (gate, up) in enumerate(hs):
        rows = slice(k * sub, (k + 1) * sub)
        act = (gate * jax.nn.sigmoid(gate) * up).astype(BF16)
        base = alpha * x_ref[rows, :] if first else o_ref[rows, :]
        y = base + jnp.dot(act, wd_ref[...], preferred_element_type=F32)
        o_ref[rows, :] = _layer_norm(y, g_ref[...], b_ref[...]) if last else y


def _ffn_ln_kernel(x_ref, wg_ref, wu_ref, wd_ref, g_ref, b_ref, o_ref, *, alpha):
    f = pl.program_id(1)
    nf = pl.num_programs(1)
    step = functools.partial(_ffn_step, x_ref, wg_ref, wu_ref, wd_ref, g_ref, b_ref, o_ref, alpha)
    pl.when(f == 0)(functools.partial(step, True, False))
    pl.when(jnp.logical_and(f > 0, f < nf - 1))(functools.partial(step, False, False))
    pl.when(f == nf - 1)(functools.partial(step, False, True))


def _ffn_ln(x, w_up, w_down, ln_g, ln_b, alpha, tf=512, tm=1024):
    m, dm = x.shape
    nf = w_down.shape[0] // tf
    return pl.pallas_call(
        functools.partial(_ffn_ln_kernel, alpha=alpha),
        grid=(m // tm, nf),
        in_specs=[
            pl.BlockSpec((tm, dm), lambda i, f: (i, 0)),
            pl.BlockSpec((dm, tf), lambda i, f: (0, f)),
            pl.BlockSpec((dm, tf), lambda i, f: (0, nf + f)),
            pl.BlockSpec((tf, dm), lambda i, f: (f, 0)),
            pl.BlockSpec((1, dm), lambda i, f: (0, 0)),
            pl.BlockSpec((1, dm), lambda i, f: (0, 0)),
        ],
        out_specs=pl.BlockSpec((tm, dm), lambda i, f: (i, 0)),
        out_shape=jax.ShapeDtypeStruct((m, dm), F32),
        compiler_params=_params(("arbitrary", "arbitrary")),
        name="ffn_ln",
    )(x, w_up, w_up, w_down, ln_g, ln_b)


def _bias_tables(rel_bias):
    def toeplitz(rows, halo, dil, cols):
        width = rows + 2 * halo
        deltas = np.arange(-(halo + rows - 1), rows + halo)
        diag = rel_bias[_t5_bucket(deltas * dil)][:, cols].T.astype(F32)
        return jnp.stack([diag[:, rows - 1 - i:rows - 1 - i + width] for i in range(rows)], axis=1)

    bias_a = []
    for g, (window, dil) in enumerate(A_GROUPS):
        assert window // (2 * dil) == A_HALO
        heads = toeplitz(A_QROWS, A_HALO, dil, slice(g * A_HEADS_PER_GROUP, (g + 1) * A_HEADS_PER_GROUP))
        masks = _band_masks(A_QROWS, A_HALO, A_HALO)
        bias_a.append(jnp.where(masks[:, None], heads[None], NEG_INF))

    heads = toeplitz(B_BLOCK, B_BLOCK, 1, slice(A_HEADS, A_HEADS + B_Q_HEADS))
    masks = _band_masks(B_BLOCK, B_BLOCK, B_HALF)
    bias_b = jnp.where(masks[:, None], heads[None], NEG_INF)
    bias_b = bias_b.reshape(3, B_KV_HEADS, B_REP, B_BLOCK, 3 * B_BLOCK).transpose(0, 1, 4, 2, 3)
    bias_b = bias_b.reshape(3, B_KV_HEADS, 3 * B_BLOCK, B_REP * B_BLOCK)
    return bias_a, bias_b


def _prep_w_in(w_in):
    qa, ka, va = (w_in[:, i * A_QKV_W:(i + 1) * A_QKV_W] for i in range(3))
    o = 3 * A_QKV_W
    qb = w_in[:, o:o + B_Q_W]
    kb = w_in[:, o + B_Q_W:o + B_Q_W + B_KV_W]
    vb = w_in[:, o + B_Q_W + B_KV_W:o + B_Q_W + 2 * B_KV_W]
    o += B_Q_W + 2 * B_KV_W
    dm = w_in.shape[0]
    ga = w_in[:, o:o + dm]
    gb = w_in[:, o + dm:o + 2 * dm]

    def grp(w, g):
        return w[:, g * A_GROUP_W:(g + 1) * A_GROUP_W]

    w_nat = jnp.concatenate([ga, gb, qb, grp(qa, 0), grp(ka, 0), grp(va, 0), kb, vb], axis=1).astype(BF16)
    w_perm = [jnp.concatenate([grp(qa, g), grp(ka, g), grp(va, g)], axis=1).astype(BF16) for g in (1, 2)]
    s_nat = np.ones((1, NAT_W), np.float32)
    s_nat[:, NAT_QB:NAT_QB + B_Q_W] = QK_SCALE
    s_nat[:, NAT_Q0:NAT_Q0 + A_GROUP_W] = QK_SCALE
    return w_nat, w_perm, jnp.asarray(s_nat)


def kernel(x, rel_bias, w_in, sink, w_pa, w_pb, w_out, ln1_g, ln1_b, w_up, w_down, ln2_g, ln2_b):
    batch, seq, dm = x.shape
    depth = w_in.shape[0]
    alpha = (2 * depth) ** 0.25
    bias_a, bias_b = _bias_tables(rel_bias)
    h = x.reshape(batch * seq, dm)
    for l in range(depth):
        w_nat, (w4, w16), s_nat = _prep_w_in(w_in[l])
        sink_row = jnp.repeat(sink[l].reshape(B_KV_HEADS, B_REP), B_BLOCK, axis=1)[:, None, :].astype(F32)
        p_nat, vbt, hb = _proj_nat(h, w_nat, s_nat)
        a4, a16 = _proj_perm(hb, w4, w16, batch, seq)
        oa = _attn_a(p_nat, a4, a16, bias_a, batch, seq)
        ob = _attn_b(p_nat, vbt, bias_b, sink_row, batch, seq)
        h = _mix_ln(h, oa, ob, p_nat, w_pa[l].astype(BF16), w_pb[l].astype(BF16), w_out[l].astype(BF16),
                    ln1_g[l][None], ln1_b[l][None], alpha)
        h = _ffn_ln(h, w_up[l].astype(BF16), w_down[l].astype(BF16), ln2_g[l][None], ln2_b[l][None], alpha)
    return h.reshape(batch, seq, dm)
```

```python
import functools
import math

import numpy as np
import jax
import jax.numpy as jnp
from jax import lax
from jax.experimental import pallas as pl
from jax.experimental.pallas import tpu as pltpu

HEAD_DIM = 128
A_GROUPS = ((128, 1), (512, 4), (2048, 16))
A_HEADS_PER_GROUP = 4
A_HEADS = A_HEADS_PER_GROUP * len(A_GROUPS)
A_HALO = 64
A_GROUP_W = A_HEADS_PER_GROUP * HEAD_DIM
A_QKV_W = A_HEADS * HEAD_DIM
B_Q_HEADS = 8
B_KV_HEADS = 2
B_REP = B_Q_HEADS // B_KV_HEADS
B_HALF = 128
B_BLOCK = 128
B_Q_W = B_Q_HEADS * HEAD_DIM
B_KV_W = B_KV_HEADS * HEAD_DIM
NUM_BUCKETS = 32
MAX_DISTANCE = 1024
LN_EPS = 1e-5
NEG_INF = -1e30
QK_SCALE = HEAD_DIM ** -0.5
A_QROWS = 128
A_BATCH = 8
B_UNROLL = 4
B_ONES_ROWS = 16
MIX_SUBTILES = 2
FFN_SUBTILES = 4

BF16 = jnp.bfloat16
F32 = jnp.float32

NAT_GA, NAT_GB, NAT_QB, NAT_Q0, NAT_K0, NAT_V0, NAT_KB, NAT_VB = 0, 2048, 4096, 5120, 5632, 6144, 6656, 6912
NAT_W = 7168

VMEM_LIMIT = 56 * 1024 * 1024


def _t5_bucket(rel):
    half = NUM_BUCKETS // 2
    max_exact = half // 2
    n = np.abs(rel)
    scaled = np.log(np.maximum(n, 1) / max_exact) / math.log(MAX_DISTANCE / max_exact)
    large = np.minimum(max_exact + (scaled * (half - max_exact)).astype(np.int64), half - 1)
    return (np.where(rel > 0, half, 0) + np.where(n < max_exact, n, large)).astype(np.int32)


def _band_masks(rows, halo, half):
    kj = np.arange(rows + 2 * halo)[None, :]
    band = np.abs(kj - halo - np.arange(rows)[:, None]) <= half
    return np.stack([band, band & (kj >= halo), band & (kj < rows + halo)])


def _layer_norm(y, g, b):
    mu = jnp.mean(y, axis=-1, keepdims=True)
    yc = y - mu
    var = jnp.mean(yc * yc, axis=-1, keepdims=True)
    return yc * lax.rsqrt(var + LN_EPS) * g + b


def _params(sem):
    return pltpu.CompilerParams(dimension_semantics=sem, vmem_limit_bytes=VMEM_LIMIT)


def _proj_nat_kernel(x_ref, w_ref, s_ref, o_ref, vt_ref, xb_ref):
    j = pl.program_id(1)

    @pl.when(j == 0)
    def _():
        xb_ref[...] = x_ref[...].astype(BF16)

    res = jnp.dot(xb_ref[...], w_ref[...], preferred_element_type=F32) * s_ref[...]
    o_ref[...] = res.astype(o_ref.dtype)

    @pl.when(j == pl.num_programs(1) - 1)
    def _():
        vt_ref[...] = res[:, res.shape[1] - B_KV_W:].T.astype(vt_ref.dtype)


def _proj_nat(x, w, colscale, tm=1024, tn=1792):
    m, k = x.shape
    n = w.shape[1]
    return pl.pallas_call(
        _proj_nat_kernel,
        grid=(m // tm, n // tn),
        in_specs=[
            pl.BlockSpec((tm, k), lambda i, j: (i, 0)),
            pl.BlockSpec((k, tn), lambda i, j: (0, j)),
            pl.BlockSpec((1, tn), lambda i, j: (0, j)),
        ],
        out_specs=[pl.BlockSpec((tm, tn), lambda i, j: (i, j)),
                   pl.BlockSpec((B_KV_W, tm), lambda i, j: (0, i)),
                   pl.BlockSpec((tm, k), lambda i, j: (i, 0))],
        out_shape=[jax.ShapeDtypeStruct((m, n), BF16), jax.ShapeDtypeStruct((B_KV_W, m), BF16),
                   jax.ShapeDtypeStruct((m, k), BF16)],
        compiler_params=_params(("arbitrary", "arbitrary")),
        name="proj_nat",
    )(x, w, colscale)


def _proj_perm_kernel(xb_ref, w4_ref, w16_ref, o4_ref, o16_ref, res_ref, tmp_ref, *, tm):
    gw = A_GROUP_W
    ncb = gw // HEAD_DIM
    q4 = tm // 4
    for g, (d, w_ref, o_ref) in enumerate(((4, w4_ref, o4_ref), (16, w16_ref, o16_ref))):
        for part in range(3):
            slot = 3 * g + part
            cols = slice(part * gw, (part + 1) * gw)
            res = jnp.dot(xb_ref[...], w_ref[:, cols], preferred_element_type=F32)
            if part == 0:
                res = res * QK_SCALE
            for cb in range(ncb):
                res_ref[slot, cb] = res[:, cb * HEAD_DIM:(cb + 1) * HEAD_DIM]
            for cb in range(ncb):
                ocols = slice(part * gw + cb * HEAD_DIM, part * gw + (cb + 1) * HEAD_DIM)
                if d == 4:
                    for r in range(4):
                        o_ref[0, r, :, ocols] = res_ref[slot, cb, pl.ds(r, q4, stride=4), :].astype(BF16)
                else:
                    for c1 in range(4):
                        tmp_ref[part, cb, c1 * q4:(c1 + 1) * q4, :] = res_ref[slot, cb, pl.ds(c1, q4, stride=4), :]
                    for c1 in range(4):
                        for c2 in range(4):
                            o_ref[0, c1 + 4 * c2, :, ocols] = (
                                tmp_ref[part, cb, pl.ds(c1 * q4 + c2, q4 // 4, stride=4), :].astype(BF16))


def _proj_perm(xb, w4, w16, batch, seq, tm=1024):
    m, k = xb.shape
    n = w4.shape[1]
    nt = seq // tm
    outs = []
    out_specs = []
    for d in (4, 16):
        outs.append(jax.ShapeDtypeStruct((batch, d, seq // d, n), BF16))
        out_specs.append(pl.BlockSpec((1, d, tm // d, n), lambda i: (i // nt, 0, i % nt, 0)))
    return pl.pallas_call(
        functools.partial(_proj_perm_kernel, tm=tm),
        grid=(m // tm,),
        in_specs=[
            pl.BlockSpec((tm, k), lambda i: (i, 0)),
            pl.BlockSpec((k, n), lambda i: (0, 0), pipeline_mode=pl.Buffered(1)),
            pl.BlockSpec((k, n), lambda i: (0, 0), pipeline_mode=pl.Buffered(1)),
        ],
        out_specs=out_specs,
        out_shape=outs,
        scratch_shapes=[pltpu.VMEM((6, A_GROUP_W // HEAD_DIM, tm, HEAD_DIM), F32),
                        pltpu.VMEM((3, A_GROUP_W // HEAD_DIM, tm, HEAD_DIM), F32)],
        compiler_params=_params(("arbitrary",)),
        name="proj_perm",
    )(xb, w4, w16)


def _edge_variant(c, nblk, t, nt):
    first = jnp.logical_and(c == 0, t == 0)
    last = jnp.logical_and(c == nblk - 1, t == nt - 1)
    return jnp.where(first, 1, jnp.where(last, 2, 0))


def _attn_a_kernel(*refs, tq):
    ins = refs[:21]
    bias_refs = refs[21:24]
    o_ref = refs[24]
    og_ref, lse_ref = refs[25:27]
    win_refs = refs[27:33]
    t = pl.program_id(1)
    nt = pl.num_programs(1)
    h = pl.program_id(2)
    halo = A_HALO
    qr = A_QROWS

    for g, (_, d) in enumerate(A_GROUPS):
        q_ref, kp_ref, k_ref, kn_ref, vp_ref, v_ref, vn_ref = ins[7 * g:7 * g + 7]
        kw_ref, vw_ref = win_refs[2 * g:2 * g + 2]
        bias_ref = bias_refs[g]
        n = tq // d
        nblk = n // qr

        for w_ref, p_ref, c_ref, n_ref in ((kw_ref, kp_ref, k_ref, kn_ref), (vw_ref, vp_ref, v_ref, vn_ref)):
            w_ref[:, 0:halo, :] = p_ref[...].reshape(d, halo, HEAD_DIM)
            w_ref[:, halo:halo + n, :] = c_ref[...].reshape(d, n, HEAD_DIM)
            w_ref[:, halo + n:, :] = n_ref[...].reshape(d, halo, HEAD_DIM)

        units = [(r, c) for r in range(d) for c in range(nblk)]
        for u0 in range(0, len(units), A_BATCH):
            qs, kws, vws, biases, dsts = [], [], [], [], []
            for r, c in units[u0:u0 + A_BATCH]:
                qrows = slice(c * qr, (c + 1) * qr)
                wrows = slice(c * qr, (c + 1) * qr + 2 * halo)
                qs.append(q_ref[qrows, :] if g == 0 else q_ref[0, r, qrows, :])
                kws.append(kw_ref[r, wrows, :])
                vws.append(vw_ref[r, wrows, :])
                biases.append(bias_ref[_edge_variant(c, nblk, t, nt), h])
                dsts.append(qrows if d == 1 else pl.ds(r + d * qr * c, qr, stride=d))
            q3 = jnp.stack(qs)
            kw3 = jnp.stack(kws)
            vw3 = jnp.stack(vws)
            s = jnp.einsum("uqd,ukd->uqk", q3, kw3, preferred_element_type=F32) + jnp.stack(biases)
            m = jnp.max(s, axis=-1, keepdims=True)
            p = jnp.exp(s - m).astype(BF16)
            vx = jnp.concatenate([vw3, jnp.ones_like(vw3)], axis=-1)
            acc = jnp.einsum("uqk,ukd->uqd", p, vx, preferred_element_type=F32)
            den = acc[..., HEAD_DIM:]
            og = acc[..., :HEAD_DIM] / den
            lse = m + jnp.log(den)
            for u, dst in enumerate(dsts):
                og_ref[g, dst, :] = og[u]
                lse_ref[g, dst, :] = lse[u]

    chunk = 256

    def merge(i, carry):
        rs = pl.ds(pl.multiple_of(i * chunk, chunk), chunk)
        ls = [lse_ref[g, rs, :] for g in range(3)]
        mx = jnp.maximum(jnp.maximum(ls[0], ls[1]), ls[2])
        num = jnp.zeros((chunk, HEAD_DIM), F32)
        dn = jnp.zeros((chunk, HEAD_DIM), F32)
        for g in range(3):
            e = jnp.exp(ls[g] - mx)
            num = num + e * og_ref[g, rs, :]
            dn = dn + e
        o_ref[rs, :] = (num / dn).astype(o_ref.dtype)
        return carry

    lax.fori_loop(0, tq // chunk, merge, 0)


def _attn_a(p_nat, a4, a16, bias, batch, seq, tq=2048):
    m = p_nat.shape[0]
    nt = seq // tq
    halo = A_HALO
    hd = HEAD_DIM
    hg = A_HEADS_PER_GROUP
    in_specs = []
    args = []

    nb_tile = tq // halo
    last_halo = m // halo - 1

    def g0_main(col0):
        return pl.BlockSpec((tq, hd), lambda b, t, h: (b * nt + t, col0 + h))

    def g0_prev(col0):
        return pl.BlockSpec((halo, hd), lambda b, t, h: (jnp.maximum((b * nt + t) * nb_tile - 1, 0), col0 + h))

    def g0_next(col0):
        return pl.BlockSpec((halo, hd),
                            lambda b, t, h: (jnp.minimum((b * nt + t + 1) * nb_tile, last_halo), col0 + h))

    q0, k0, v0 = NAT_Q0 // hd, NAT_K0 // hd, NAT_V0 // hd
    in_specs += [g0_main(q0), g0_prev(k0), g0_main(k0), g0_next(k0), g0_prev(v0), g0_main(v0), g0_next(v0)]
    args += [p_nat] * 7

    for arr, d in ((a4, 4), (a16, 16)):
        n = tq // d
        nb = n // halo
        lastb = seq // d // halo - 1

        def main(part, d=d, n=n):
            return pl.BlockSpec((1, d, n, hd), lambda b, t, h: (b, 0, t, part * hg + h))

        def prev(part, d=d, nb=nb):
            return pl.BlockSpec((1, d, halo, hd),
                                lambda b, t, h: (b, 0, jnp.maximum(t * nb - 1, 0), part * hg + h))

        def nxt(part, d=d, nb=nb, lastb=lastb):
            return pl.BlockSpec((1, d, halo, hd),
                                lambda b, t, h: (b, 0, jnp.minimum((t + 1) * nb, lastb), part * hg + h))

        in_specs += [main(0), prev(1), main(1), nxt(1), prev(2), main(2), nxt(2)]
        args += [arr] * 7

    for tab in bias:
        in_specs.append(pl.BlockSpec(tab.shape, lambda b, t, h: (0, 0, 0, 0), pipeline_mode=pl.Buffered(1)))
        args.append(tab)

    return pl.pallas_call(
        functools.partial(_attn_a_kernel, tq=tq),
        grid=(batch, nt, hg),
        in_specs=in_specs,
        out_specs=pl.BlockSpec((tq, hd), lambda b, t, h: (b * nt + t, h)),
        out_shape=jax.ShapeDtypeStruct((m, A_GROUP_W), BF16),
        scratch_shapes=(
            [pltpu.VMEM((3, tq, hd), F32)] * 2
            + [pltpu.VMEM((d, tq // d + 2 * halo, hd), BF16) for _, d in A_GROUPS for _ in range(2)]),
        compiler_params=_params(("arbitrary", "arbitrary", "arbitrary")),
        name="attn_a",
    )(*args)


def _attn_b_kernel(q_ref, kp_ref, k_ref, kn_ref, vp_ref, v_ref, vn_ref, bias_ref, sink_ref, o_ref,
                   kw_ref, vw_ref, *, tq):
    t = pl.program_id(1)
    nt = pl.num_programs(1)
    blk = B_BLOCK
    nblk = tq // blk

    kw_ref[0:blk, :] = kp_ref[...]
    kw_ref[blk:blk + tq, :] = k_ref[...]
    kw_ref[blk + tq:, :] = kn_ref[...]
    vw_ref[:, 0:blk] = vp_ref[...]
    vw_ref[:, blk:blk + tq] = v_ref[...]
    vw_ref[:, blk + tq:] = vn_ref[...]
    ones = jnp.ones((B_ONES_ROWS, 3 * blk), BF16)

    def block(it, carry):
        units = []
        for j in range(B_UNROLL):
            c = it * B_UNROLL + j
            variant = _edge_variant(c, nblk, t, nt)
            rows = pl.ds(pl.multiple_of(c * blk, blk), blk)
            wrows = pl.ds(pl.multiple_of(c * blk, blk), 3 * blk)
            for kvh in range(B_KV_HEADS):
                units.append((kvh, variant, rows, wrows))
        scores = []
        for kvh, variant, rows, wrows in units:
            kcols = slice(kvh * HEAD_DIM, (kvh + 1) * HEAD_DIM)
            q4 = jnp.concatenate(
                [q_ref[rows, (kvh * B_REP + rep) * HEAD_DIM:(kvh * B_REP + rep + 1) * HEAD_DIM]
                 for rep in range(B_REP)], axis=0)
            scores.append(lax.dot_general(kw_ref[wrows, kcols], q4, (((1,), (1,)), ((), ())),
                                          preferred_element_type=F32))
        probs = []
        for (kvh, variant, rows, wrows), s in zip(units, scores):
            s = s + bias_ref[variant, kvh]
            sink = sink_ref[kvh]
            m = jnp.maximum(jnp.max(s, axis=0, keepdims=True), sink)
            probs.append((jnp.exp(s - m).astype(BF16), jnp.exp(sink - m)))
        for (kvh, variant, rows, wrows), (p, psink) in zip(units, probs):
            kcols = slice(kvh * HEAD_DIM, (kvh + 1) * HEAD_DIM)
            vx = jnp.concatenate([vw_ref[kcols, wrows], ones], axis=0)
            acc = jnp.dot(vx, p, preferred_element_type=F32)
            den = acc[HEAD_DIM:HEAD_DIM + 1, :] + psink
            o = (acc[:HEAD_DIM, :] / den).astype(o_ref.dtype)
            for rep in range(B_REP):
                h = kvh * B_REP + rep
                o_ref[h * HEAD_DIM:(h + 1) * HEAD_DIM, rows] = o[:, rep * blk:(rep + 1) * blk]
        return carry

    lax.fori_loop(0, nblk // B_UNROLL, block, 0)


def _attn_b(p_nat, vbt, bias, sink, batch, seq, tq=1024):
    m = p_nat.shape[0]
    nt = seq // tq
    blk = B_BLOCK
    nb_tile = tq // blk
    lastb = m // blk - 1
    q_col = NAT_QB // B_Q_W
    k_col = NAT_KB // B_KV_W

    def first(b, t):
        return jnp.maximum((b * nt + t) * nb_tile - 1, 0)

    def after(b, t):
        return jnp.minimum((b * nt + t + 1) * nb_tile, lastb)

    return pl.pallas_call(
        functools.partial(_attn_b_kernel, tq=tq),
        grid=(batch, nt),
        in_specs=[
            pl.BlockSpec((tq, B_Q_W), lambda b, t: (b * nt + t, q_col)),
            pl.BlockSpec((blk, B_KV_W), lambda b, t: (first(b, t), k_col)),
            pl.BlockSpec((tq, B_KV_W), lambda b, t: (b * nt + t, k_col)),
            pl.BlockSpec((blk, B_KV_W), lambda b, t: (after(b, t), k_col)),
            pl.BlockSpec((B_KV_W, blk), lambda b, t: (0, first(b, t))),
            pl.BlockSpec((B_KV_W, tq), lambda b, t: (0, b * nt + t)),
            pl.BlockSpec((B_KV_W, blk), lambda b, t: (0, after(b, t))),
            pl.BlockSpec(bias.shape, lambda b, t: (0, 0, 0, 0)),
            pl.BlockSpec(sink.shape, lambda b, t: (0, 0, 0)),
        ],
        out_specs=pl.BlockSpec((B_Q_W, tq), lambda b, t: (0, b * nt + t)),
        out_shape=jax.ShapeDtypeStruct((B_Q_W, m), BF16),
        scratch_shapes=[pltpu.VMEM((tq + 2 * blk, B_KV_W), BF16), pltpu.VMEM((B_KV_W, tq + 2 * blk), BF16)],
        compiler_params=_params(("arbitrary", "arbitrary")),
        name="attn_b",
    )(p_nat, p_nat, p_nat, p_nat, vbt, vbt, vbt, bias, sink)


def _mix_ln_kernel(x_ref, oa_ref, ob_ref, ga_ref, gb_ref, wpa_ref, wpb_ref, wo_ref, g_ref, b_ref, o_ref, *, alpha):
    sub = x_ref.shape[0] // MIX_SUBTILES
    ys = []
    for k in range(MIX_SUBTILES):
        rows = slice(k * sub, (k + 1) * sub)
        ya = jnp.dot(oa_ref[rows, :], wpa_ref[...], preferred_element_type=F32)
        yb = lax.dot_general(ob_ref[:, rows], wpb_ref[...], (((0,), (0,)), ((), ())),
                             preferred_element_type=F32)
        ys.append((ya, yb))
    for k, (ya, yb) in enumerate(ys):
        rows = slice(k * sub, (k + 1) * sub)
        merged = (jax.nn.sigmoid(ga_ref[rows, :].astype(F32)) * ya
                  + jax.nn.sigmoid(gb_ref[rows, :].astype(F32)) * yb)
        z = jnp.dot(merged.astype(BF16), wo_ref[...], preferred_element_type=F32)
        o_ref[rows, :] = _layer_norm(alpha * x_ref[rows, :] + z, g_ref[...], b_ref[...])


def _mix_ln(x, oa, ob, p_nat, w_pa, w_pb, w_out, ln_g, ln_b, alpha, tm=512):
    m, dm = x.shape

    def const(arr):
        return pl.BlockSpec(arr.shape, lambda i: (0, 0), pipeline_mode=pl.Buffered(1))

    return pl.pallas_call(
        functools.partial(_mix_ln_kernel, alpha=alpha),
        grid=(m // tm,),
        in_specs=[
            pl.BlockSpec((tm, dm), lambda i: (i, 0)),
            pl.BlockSpec((tm, oa.shape[1]), lambda i: (i, 0)),
            pl.BlockSpec((ob.shape[0], tm), lambda i: (0, i)),
            pl.BlockSpec((tm, dm), lambda i: (i, NAT_GA // dm)),
            pl.BlockSpec((tm, dm), lambda i: (i, NAT_GB // dm)),
            const(w_pa), const(w_pb), const(w_out), const(ln_g), const(ln_b),
        ],
        out_specs=pl.BlockSpec((tm, dm), lambda i: (i, 0)),
        out_shape=jax.ShapeDtypeStruct((m, dm), F32),
        compiler_params=_params(("arbitrary",)),
        name="mix_ln",
    )(x, oa, ob, p_nat, p_nat, w_pa, w_pb, w_out, ln_g, ln_b)


def _ffn_step(x_ref, wg_ref, wu_ref, wd_ref, g_ref, b_ref, o_ref, xb_ref, alpha, first, last):
    sub = x_ref.shape[0] // FFN_SUBTILES
    hs = []
    for k in range(FFN_SUBTILES):
        rows = slice(k * sub, (k + 1) * sub)
        if first:
            xb_ref[rows, :] = x_ref[rows, :].astype(BF16)
        xb = xb_ref[rows, :]
        hs.append((jnp.dot(xb, wg_ref[...], preferred_element_type=F32),
                   jnp.dot(xb, wu_ref[...], preferred_element_type=F32)))
    for k, (gate, up) in enumerate(hs):
        rows = slice(k * sub, (k + 1) * sub)
        act = (gate * jax.nn.sigmoid(gate) * up).astype(BF16)
        base = alpha * x_ref[rows, :] if first else o_ref[rows, :]
        y = base + jnp.dot(act, wd_ref[...], preferred_element_type=F32)
        o_ref[rows, :] = _layer_norm(y, g_ref[...], b_ref[...]) if last else y


def _ffn_ln_kernel(x_ref, wg_ref, wu_ref, wd_ref, g_ref, b_ref, o_ref, xb_ref, *, alpha):
    f = pl.program_id(1)
    nf = pl.num_programs(1)
    step = functools.partial(_ffn_step, x_ref, wg_ref, wu_ref, wd_ref, g_ref, b_ref, o_ref, xb_ref, alpha)
    pl.when(f == 0)(functools.partial(step, True, False))
    pl.when(jnp.logical_and(f > 0, f < nf - 1))(functools.partial(step, False, False))
    pl.when(f == nf - 1)(functools.partial(step, False, True))


def _ffn_ln(x, w_up, w_down, ln_g, ln_b, alpha, tf=512, tm=1024):
    m, dm = x.shape
    nf = w_down.shape[0] // tf
    return pl.pallas_call(
        functools.partial(_ffn_ln_kernel, alpha=alpha),
        grid=(m // tm, nf),
        in_specs=[
            pl.BlockSpec((tm, dm), lambda i, f: (i, 0)),
            pl.BlockSpec((dm, tf), lambda i, f: (0, f)),
            pl.BlockSpec((dm, tf), lambda i, f: (0, nf + f)),
            pl.BlockSpec((tf, dm), lambda i, f: (f, 0)),
            pl.BlockSpec((1, dm), lambda i, f: (0, 0)),
            pl.BlockSpec((1, dm), lambda i, f: (0, 0)),
        ],
        out_specs=pl.BlockSpec((tm, dm), lambda i, f: (i, 0)),
        out_shape=jax.ShapeDtypeStruct((m, dm), F32),
        scratch_shapes=[pltpu.VMEM((tm, dm), BF16)],
        compiler_params=_params(("arbitrary", "arbitrary")),
        name="ffn_ln",
    )(x, w_up, w_up, w_down, ln_g, ln_b)


def _bias_tables(rel_bias):
    def toeplitz(rows, halo, dil, cols):
        width = rows + 2 * halo
        deltas = np.arange(-(halo + rows - 1), rows + halo)
        diag = rel_bias[_t5_bucket(deltas * dil)][:, cols].T.astype(F32)
        return jnp.stack([diag[:, rows - 1 - i:rows - 1 - i + width] for i in range(rows)], axis=1)

    bias_a = []
    for g, (window, dil) in enumerate(A_GROUPS):
        assert window // (2 * dil) == A_HALO
        heads = toeplitz(A_QROWS, A_HALO, dil, slice(g * A_HEADS_PER_GROUP, (g + 1) * A_HEADS_PER_GROUP))
        masks = _band_masks(A_QROWS, A_HALO, A_HALO)
        bias_a.append(jnp.where(masks[:, None], heads[None], NEG_INF))

    heads = toeplitz(B_BLOCK, B_BLOCK, 1, slice(A_HEADS, A_HEADS + B_Q_HEADS))
    masks = _band_masks(B_BLOCK, B_BLOCK, B_HALF)
    bias_b = jnp.where(masks[:, None], heads[None], NEG_INF)
    bias_b = bias_b.reshape(3, B_KV_HEADS, B_REP, B_BLOCK, 3 * B_BLOCK).transpose(0, 1, 4, 2, 3)
    bias_b = bias_b.reshape(3, B_KV_HEADS, 3 * B_BLOCK, B_REP * B_BLOCK)
    return bias_a, bias_b


def _prep_w_in(w_in):
    qa, ka, va = (w_in[:, i * A_QKV_W:(i + 1) * A_QKV_W] for i in range(3))
    o = 3 * A_QKV_W
    qb = w_in[:, o:o + B_Q_W]
    kb = w_in[:, o + B_Q_W:o + B_Q_W + B_KV_W]
    vb = w_in[:, o + B_Q_W + B_KV_W:o + B_Q_W + 2 * B_KV_W]
    o += B_Q_W + 2 * B_KV_W
    dm = w_in.shape[0]
    ga = w_in[:, o:o + dm]
    gb = w_in[:, o + dm:o + 2 * dm]

    def grp(w, g):
        return w[:, g * A_GROUP_W:(g + 1) * A_GROUP_W]

    w_nat = jnp.concatenate([ga, gb, qb, grp(qa, 0), grp(ka, 0), grp(va, 0), kb, vb], axis=1).astype(BF16)
    w_perm = [jnp.concatenate([grp(qa, g), grp(ka, g), grp(va, g)], axis=1).astype(BF16) for g in (1, 2)]
    s_nat = np.ones((1, NAT_W), np.float32)
    s_nat[:, NAT_QB:NAT_QB + B_Q_W] = QK_SCALE
    s_nat[:, NAT_Q0:NAT_Q0 + A_GROUP_W] = QK_SCALE
    return w_nat, w_perm, jnp.asarray(s_nat)


def kernel(x, rel_bias, w_in, sink, w_pa, w_pb, w_out, ln1_g, ln1_b, w_up, w_down, ln2_g, ln2_b):
    batch, seq, dm = x.shape
    depth = w_in.shape[0]
    alpha = (2 * depth) ** 0.25
    bias_a, bias_b = _bias_tables(rel_bias)
    h = x.reshape(batch * seq, dm)
    for l in range(depth):
        w_nat, (w4, w16), s_nat = _prep_w_in(w_in[l])
        sink_row = jnp.repeat(sink[l].reshape(B_KV_HEADS, B_REP), B_BLOCK, axis=1)[:, None, :].astype(F32)
        p_nat, vbt, hb = _proj_nat(h, w_nat, s_nat)
        a4, a16 = _proj_perm(hb, w4, w16, batch, seq)
        oa = _attn_a(p_nat, a4, a16, bias_a, batch, seq)
        ob = _attn_b(p_nat, vbt, bias_b, sink_row, batch, seq)
        h = _mix_ln(h, oa, ob, p_nat, w_pa[l].astype(BF16), w_pb[l].astype(BF16), w_out[l].astype(BF16),
                    ln1_g[l][None], ln1_b[l][None], alpha)
        h = _ffn_ln(h, w_up[l].astype(BF16), w_down[l].astype(BF16), ln2_g[l][None], ln2_b[l][None], alpha)
    return h.reshape(batch, seq, dm)
```

```python
import functools
import math

import numpy as np
import jax
import jax.numpy as jnp
from jax import lax
from jax.experimental import pallas as pl
from jax.experimental.pallas import tpu as pltpu

HEAD_DIM = 128
A_GROUPS = ((128, 1), (512, 4), (2048, 16))
A_HEADS_PER_GROUP = 4
A_HEADS = A_HEADS_PER_GROUP * len(A_GROUPS)
A_HALO = 64
A_GROUP_W = A_HEADS_PER_GROUP * HEAD_DIM
A_QKV_W = A_HEADS * HEAD_DIM
B_Q_HEADS = 8
B_KV_HEADS = 2
B_REP = B_Q_HEADS // B_KV_HEADS
B_HALF = 128
B_BLOCK = 128
B_Q_W = B_Q_HEADS * HEAD_DIM
B_KV_W = B_KV_HEADS * HEAD_DIM
NUM_BUCKETS = 32
MAX_DISTANCE = 1024
LN_EPS = 1e-5
NEG_INF = -1e30
QK_SCALE = HEAD_DIM ** -0.5
A_QROWS = 128
A_BATCH = 8
B_UNROLL = 4
B_ONES_ROWS = 16
PROJ_SUBTILES = 4
MIX_SUBTILES = 2
FFN_SUBTILES = 4

BF16 = jnp.bfloat16
F32 = jnp.float32

D_MODEL = 2048
_NAT_WIDTHS = (D_MODEL, D_MODEL, B_Q_W, A_GROUP_W, A_GROUP_W, A_GROUP_W, B_KV_W, B_KV_W)
NAT_GA, NAT_GB, NAT_QB, NAT_Q0, NAT_K0, NAT_V0, NAT_KB, NAT_VB = (
    int(c) for c in np.cumsum((0,) + _NAT_WIDTHS[:-1]))
NAT_W = sum(_NAT_WIDTHS)

VMEM_LIMIT = 56 * 1024 * 1024


def _t5_bucket(rel):
    half = NUM_BUCKETS // 2
    max_exact = half // 2
    n = np.abs(rel)
    scaled = np.log(np.maximum(n, 1) / max_exact) / math.log(MAX_DISTANCE / max_exact)
    large = np.minimum(max_exact + (scaled * (half - max_exact)).astype(np.int64), half - 1)
    return (np.where(rel > 0, half, 0) + np.where(n < max_exact, n, large)).astype(np.int32)


def _band_masks(rows, halo, half):
    kj = np.arange(rows + 2 * halo)[None, :]
    band = np.abs(kj - halo - np.arange(rows)[:, None]) <= half
    return np.stack([band, band & (kj >= halo), band & (kj < rows + halo)])


def _layer_norm(y, g, b):
    mu = jnp.mean(y, axis=-1, keepdims=True)
    yc = y - mu
    var = jnp.mean(yc * yc, axis=-1, keepdims=True)
    return yc * lax.rsqrt(var + LN_EPS) * g + b


def _params(sem):
    return pltpu.CompilerParams(dimension_semantics=sem, vmem_limit_bytes=VMEM_LIMIT)


def _proj_nat_step(x_ref, w_ref, s_ref, o_ref, vt_ref, xb_ref, first, last):
    sub = x_ref.shape[0] // PROJ_SUBTILES
    for k in range(PROJ_SUBTILES):
        rows = slice(k * sub, (k + 1) * sub)
        if first:
            xb_ref[rows, :] = x_ref[rows, :].astype(BF16)
        res = jnp.dot(xb_ref[rows, :], w_ref[...], preferred_element_type=F32) * s_ref[...]
        o_ref[rows, :] = res.astype(o_ref.dtype)
        if last:
            vt_ref[:, rows] = res[:, res.shape[1] - B_KV_W:].T.astype(vt_ref.dtype)


def _proj_nat_kernel(x_ref, w_ref, s_ref, o_ref, vt_ref, xb_ref):
    j = pl.program_id(1)
    nj = pl.num_programs(1)
    step = functools.partial(_proj_nat_step, x_ref, w_ref, s_ref, o_ref, vt_ref, xb_ref)
    pl.when(j == 0)(functools.partial(step, True, False))
    pl.when(jnp.logical_and(j > 0, j < nj - 1))(functools.partial(step, False, False))
    pl.when(j == nj - 1)(functools.partial(step, False, True))


def _proj_nat(x, w, colscale, tm=1024, tn=1792):
    m, k = x.shape
    n = w.shape[1]
    return pl.pallas_call(
        _proj_nat_kernel,
        grid=(m // tm, n // tn),
        in_specs=[
            pl.BlockSpec((tm, k), lambda i, j: (i, 0)),
            pl.BlockSpec((k, tn), lambda i, j: (0, j)),
            pl.BlockSpec((1, tn), lambda i, j: (0, j)),
        ],
        out_specs=[pl.BlockSpec((tm, tn), lambda i, j: (i, j)),
                   pl.BlockSpec((B_KV_W, tm), lambda i, j: (0, i)),
                   pl.BlockSpec((tm, k), lambda i, j: (i, 0))],
        out_shape=[jax.ShapeDtypeStruct((m, n), BF16), jax.ShapeDtypeStruct((B_KV_W, m), BF16),
                   jax.ShapeDtypeStruct((m, k), BF16)],
        compiler_params=_params(("arbitrary", "arbitrary")),
        name="proj_nat",
    )(x, w, colscale)


def _proj_perm_kernel(xb_ref, w4_ref, w16_ref, o4_ref, o16_ref, res_ref, tmp_ref, *, tm):
    gw = A_GROUP_W
    ncb = gw // HEAD_DIM
    q4 = tm // 4
    for g, (d, w_ref, o_ref) in enumerate(((4, w4_ref, o4_ref), (16, w16_ref, o16_ref))):
        for part in range(3):
            slot = 3 * g + part
            cols = slice(part * gw, (part + 1) * gw)
            res = jnp.dot(xb_ref[...], w_ref[:, cols], preferred_element_type=F32)
            if part == 0:
                res = res * QK_SCALE
            for cb in range(ncb):
                res_ref[slot, cb] = res[:, cb * HEAD_DIM:(cb + 1) * HEAD_DIM]
            for cb in range(ncb):
                ocols = slice(part * gw + cb * HEAD_DIM, part * gw + (cb + 1) * HEAD_DIM)
                if d == 4:
                    for r in range(4):
                        o_ref[0, r, :, ocols] = res_ref[slot, cb, pl.ds(r, q4, stride=4), :].astype(BF16)
                else:
                    for c1 in range(4):
                        tmp_ref[part, cb, c1 * q4:(c1 + 1) * q4, :] = res_ref[slot, cb, pl.ds(c1, q4, stride=4), :]
                    for c1 in range(4):
                        for c2 in range(4):
                            o_ref[0, c1 + 4 * c2, :, ocols] = (
                                tmp_ref[part, cb, pl.ds(c1 * q4 + c2, q4 // 4, stride=4), :].astype(BF16))


def _proj_perm(xb, w4, w16, batch, seq, tm=1024):
    m, k = xb.shape
    n = w4.shape[1]
    nt = seq // tm
    outs = []
    out_specs = []
    for d in (4, 16):
        outs.append(jax.ShapeDtypeStruct((batch, d, seq // d, n), BF16))
        out_specs.append(pl.BlockSpec((1, d, tm // d, n), lambda i: (i // nt, 0, i % nt, 0)))
    return pl.pallas_call(
        functools.partial(_proj_perm_kernel, tm=tm),
        grid=(m // tm,),
        in_specs=[
            pl.BlockSpec((tm, k), lambda i: (i, 0)),
            pl.BlockSpec((k, n), lambda i: (0, 0), pipeline_mode=pl.Buffered(1)),
            pl.BlockSpec((k, n), lambda i: (0, 0), pipeline_mode=pl.Buffered(1)),
        ],
        out_specs=out_specs,
        out_shape=outs,
        scratch_shapes=[pltpu.VMEM((6, A_GROUP_W // HEAD_DIM, tm, HEAD_DIM), F32),
                        pltpu.VMEM((3, A_GROUP_W // HEAD_DIM, tm, HEAD_DIM), F32)],
        compiler_params=_params(("arbitrary",)),
        name="proj_perm",
    )(xb, w4, w16)


def _edge_variant(c, nblk, t, nt):
    first = jnp.logical_and(c == 0, t == 0)
    last = jnp.logical_and(c == nblk - 1, t == nt - 1)
    return jnp.where(first, 1, jnp.where(last, 2, 0))


def _attn_a_kernel(*refs, tq):
    ins = refs[:21]
    bias_refs = refs[21:24]
    o_ref = refs[24]
    og_ref, lse_ref = refs[25:27]
    win_refs = refs[27:33]
    t = pl.program_id(1)
    nt = pl.num_programs(1)
    h = pl.program_id(2)
    halo = A_HALO
    qr = A_QROWS

    for g, (_, d) in enumerate(A_GROUPS):
        q_ref, kp_ref, k_ref, kn_ref, vp_ref, v_ref, vn_ref = ins[7 * g:7 * g + 7]
        kw_ref, vw_ref = win_refs[2 * g:2 * g + 2]
        bias_ref = bias_refs[g]
        n = tq // d
        nblk = n // qr

        for w_ref, p_ref, c_ref, n_ref in ((kw_ref, kp_ref, k_ref, kn_ref), (vw_ref, vp_ref, v_ref, vn_ref)):
            w_ref[:, 0:halo, :] = p_ref[...].reshape(d, halo, HEAD_DIM)
            w_ref[:, halo:halo + n, :] = c_ref[...].reshape(d, n, HEAD_DIM)
            w_ref[:, halo + n:, :] = n_ref[...].reshape(d, halo, HEAD_DIM)

        units = [(r, c) for r in range(d) for c in range(nblk)]
        for u0 in range(0, len(units), A_BATCH):
            qs, kws, vws, biases, dsts = [], [], [], [], []
            for r, c in units[u0:u0 + A_BATCH]:
                qrows = slice(c * qr, (c + 1) * qr)
                wrows = slice(c * qr, (c + 1) * qr + 2 * halo)
                qs.append(q_ref[qrows, :] if g == 0 else q_ref[0, r, qrows, :])
                kws.append(kw_ref[r, wrows, :])
                vws.append(vw_ref[r, wrows, :])
                biases.append(bias_ref[_edge_variant(c, nblk, t, nt), h])
                dsts.append(qrows if d == 1 else pl.ds(r + d * qr * c, qr, stride=d))
            q3 = jnp.stack(qs)
            kw3 = jnp.stack(kws)
            vw3 = jnp.stack(vws)
            s = jnp.einsum("uqd,ukd->uqk", q3, kw3, preferred_element_type=F32) + jnp.stack(biases)
            m = jnp.max(s, axis=-1, keepdims=True)
            p = jnp.exp(s - m).astype(BF16)
            vx = jnp.concatenate([vw3, jnp.ones_like(vw3)], axis=-1)
            acc = jnp.einsum("uqk,ukd->uqd", p, vx, preferred_element_type=F32)
            den = acc[..., HEAD_DIM:]
            og = acc[..., :HEAD_DIM] / den
            lse = m + jnp.log(den)
            for u, dst in enumerate(dsts):
                og_ref[g, dst, :] = og[u]
                lse_ref[g, dst, :] = lse[u]

    chunk = 256

    def merge(i, carry):
        rs = pl.ds(pl.multiple_of(i * chunk, chunk), chunk)
        ls = [lse_ref[g, rs, :] for g in range(3)]
        mx = jnp.maximum(jnp.maximum(ls[0], ls[1]), ls[2])
        num = jnp.zeros((chunk, HEAD_DIM), F32)
        dn = jnp.zeros((chunk, HEAD_DIM), F32)
        for g in range(3):
            e = jnp.exp(ls[g] - mx)
            num = num + e * og_ref[g, rs, :]
            dn = dn + e
        o_ref[rs, :] = (num / dn).astype(o_ref.dtype)
        return carry

    lax.fori_loop(0, tq // chunk, merge, 0)


def _attn_a(p_nat, a4, a16, bias, batch, seq, tq=4096):
    m = p_nat.shape[0]
    nt = seq // tq
    halo = A_HALO
    hd = HEAD_DIM
    hg = A_HEADS_PER_GROUP
    in_specs = []
    args = []

    nb_tile = tq // halo
    last_halo = m // halo - 1

    def g0_main(col0):
        return pl.BlockSpec((tq, hd), lambda b, t, h: (b * nt + t, col0 + h))

    def g0_prev(col0):
        return pl.BlockSpec((halo, hd), lambda b, t, h: (jnp.maximum((b * nt + t) * nb_tile - 1, 0), col0 + h))

    def g0_next(col0):
        return pl.BlockSpec((halo, hd),
                            lambda b, t, h: (jnp.minimum((b * nt + t + 1) * nb_tile, last_halo), col0 + h))

    q0, k0, v0 = NAT_Q0 // hd, NAT_K0 // hd, NAT_V0 // hd
    in_specs += [g0_main(q0), g0_prev(k0), g0_main(k0), g0_next(k0), g0_prev(v0), g0_main(v0), g0_next(v0)]
    args += [p_nat] * 7

    for arr, d in ((a4, 4), (a16, 16)):
        n = tq // d
        nb = n // halo
        lastb = seq // d // halo - 1

        def main(part, d=d, n=n):
            return pl.BlockSpec((1, d, n, hd), lambda b, t, h: (b, 0, t, part * hg + h))

        def prev(part, d=d, nb=nb):
            return pl.BlockSpec((1, d, halo, hd),
                                lambda b, t, h: (b, 0, jnp.maximum(t * nb - 1, 0), part * hg + h))

        def nxt(part, d=d, nb=nb, lastb=lastb):
            return pl.BlockSpec((1, d, halo, hd),
                                lambda b, t, h: (b, 0, jnp.minimum((t + 1) * nb, lastb), part * hg + h))

        in_specs += [main(0), prev(1), main(1), nxt(1), prev(2), main(2), nxt(2)]
        args += [arr] * 7

    for tab in bias:
        in_specs.append(pl.BlockSpec(tab.shape, lambda b, t, h: (0, 0, 0, 0), pipeline_mode=pl.Buffered(1)))
        args.append(tab)

    return pl.pallas_call(
        functools.partial(_attn_a_kernel, tq=tq),
        grid=(batch, nt, hg),
        in_specs=in_specs,
        out_specs=pl.BlockSpec((tq, hd), lambda b, t, h: (b * nt + t, h)),
        out_shape=jax.ShapeDtypeStruct((m, A_GROUP_W), BF16),
        scratch_shapes=(
            [pltpu.VMEM((3, tq, hd), F32)] * 2
            + [pltpu.VMEM((d, tq // d + 2 * halo, hd), BF16) for _, d in A_GROUPS for _ in range(2)]),
        compiler_params=_params(("arbitrary", "arbitrary", "arbitrary")),
        name="attn_a",
    )(*args)


def _attn_b_kernel(q_ref, kp_ref, k_ref, kn_ref, vp_ref, v_ref, vn_ref, bias_ref, sink_ref, o_ref,
                   kw_ref, vw_ref, *, tq):
    t = pl.program_id(1)
    nt = pl.num_programs(1)
    blk = B_BLOCK
    nblk = tq // blk

    kw_ref[0:blk, :] = kp_ref[...]
    kw_ref[blk:blk + tq, :] = k_ref[...]
    kw_ref[blk + tq:, :] = kn_ref[...]
    vw_ref[:, 0:blk] = vp_ref[...]
    vw_ref[:, blk:blk + tq] = v_ref[...]
    vw_ref[:, blk + tq:] = vn_ref[...]
    ones = jnp.ones((B_ONES_ROWS, 3 * blk), BF16)

    def block(it, carry):
        units = []
        for j in range(B_UNROLL):
            c = it * B_UNROLL + j
            variant = _edge_variant(c, nblk, t, nt)
            rows = pl.ds(pl.multiple_of(c * blk, blk), blk)
            wrows = pl.ds(pl.multiple_of(c * blk, blk), 3 * blk)
            for kvh in range(B_KV_HEADS):
                units.append((kvh, variant, rows, wrows))
        scores = []
        for kvh, variant, rows, wrows in units:
            kcols = slice(kvh * HEAD_DIM, (kvh + 1) * HEAD_DIM)
            q4 = jnp.concatenate(
                [q_ref[rows, (kvh * B_REP + rep) * HEAD_DIM:(kvh * B_REP + rep + 1) * HEAD_DIM]
                 for rep in range(B_REP)], axis=0)
            scores.append(lax.dot_general(kw_ref[wrows, kcols], q4, (((1,), (1,)), ((), ())),
                                          preferred_element_type=F32))
        probs = []
        for (kvh, variant, rows, wrows), s in zip(units, scores):
            s = s + bias_ref[variant, kvh]
            sink = sink_ref[kvh]
            m = jnp.maximum(jnp.max(s, axis=0, keepdims=True), sink)
            probs.append((jnp.exp(s - m).astype(BF16), jnp.exp(sink - m)))
        for (kvh, variant, rows, wrows), (p, psink) in zip(units, probs):
            kcols = slice(kvh * HEAD_DIM, (kvh + 1) * HEAD_DIM)
            vx = jnp.concatenate([vw_ref[kcols, wrows], ones], axis=0)
            acc = jnp.dot(vx, p, preferred_element_type=F32)
            den = acc[HEAD_DIM:HEAD_DIM + 1, :] + psink
            o = (acc[:HEAD_DIM, :] / den).astype(o_ref.dtype)
            for rep in range(B_REP):
                h = kvh * B_REP + rep
                o_ref[h * HEAD_DIM:(h + 1) * HEAD_DIM, rows] = o[:, rep * blk:(rep + 1) * blk]
        return carry

    lax.fori_loop(0, nblk // B_UNROLL, block, 0)


def _attn_b(p_nat, vbt, bias, sink, batch, seq, tq=2048):
    m = p_nat.shape[0]
    nt = seq // tq
    blk = B_BLOCK
    nb_tile = tq // blk
    lastb = m // blk - 1
    q_col = NAT_QB // B_Q_W
    k_col = NAT_KB // B_KV_W

    def first(b, t):
        return jnp.maximum((b * nt + t) * nb_tile - 1, 0)

    def after(b, t):
        return jnp.minimum((b * nt + t + 1) * nb_tile, lastb)

    return pl.pallas_call(
        functools.partial(_attn_b_kernel, tq=tq),
        grid=(batch, nt),
        in_specs=[
            pl.BlockSpec((tq, B_Q_W), lambda b, t: (b * nt + t, q_col)),
            pl.BlockSpec((blk, B_KV_W), lambda b, t: (first(b, t), k_col)),
            pl.BlockSpec((tq, B_KV_W), lambda b, t: (b * nt + t, k_col)),
            pl.BlockSpec((blk, B_KV_W), lambda b, t: (after(b, t), k_col)),
            pl.BlockSpec((B_KV_W, blk), lambda b, t: (0, first(b, t))),
            pl.BlockSpec((B_KV_W, tq), lambda b, t: (0, b * nt + t)),
            pl.BlockSpec((B_KV_W, blk), lambda b, t: (0, after(b, t))),
            pl.BlockSpec(bias.shape, lambda b, t: (0, 0, 0, 0)),
            pl.BlockSpec(sink.shape, lambda b, t: (0, 0, 0)),
        ],
        out_specs=pl.BlockSpec((B_Q_W, tq), lambda b, t: (0, b * nt + t)),
        out_shape=jax.ShapeDtypeStruct((B_Q_W, m), BF16),
        scratch_shapes=[pltpu.VMEM((tq + 2 * blk, B_KV_W), BF16), pltpu.VMEM((B_KV_W, tq + 2 * blk), BF16)],
        compiler_params=_params(("arbitrary", "arbitrary")),
        name="attn_b",
    )(p_nat, p_nat, p_nat, p_nat, vbt, vbt, vbt, bias, sink)


def _mix_ln_kernel(x_ref, oa_ref, ob_ref, ga_ref, gb_ref, wpa_ref, wpb_ref, wo_ref, g_ref, b_ref, o_ref, *, alpha):
    sub = x_ref.shape[0] // MIX_SUBTILES
    ys = []
    for k in range(MIX_SUBTILES):
        rows = slice(k * sub, (k + 1) * sub)
        ya = jnp.dot(oa_ref[rows, :], wpa_ref[...], preferred_element_type=F32)
        yb = lax.dot_general(ob_ref[:, rows], wpb_ref[...], (((0,), (0,)), ((), ())),
                             preferred_element_type=F32)
        ys.append((ya, yb))
    for k, (ya, yb) in enumerate(ys):
        rows = slice(k * sub, (k + 1) * sub)
        merged = (jax.nn.sigmoid(ga_ref[rows, :].astype(F32)) * ya
                  + jax.nn.sigmoid(gb_ref[rows, :].astype(F32)) * yb)
        z = jnp.dot(merged.astype(BF16), wo_ref[...], preferred_element_type=F32)
        o_ref[rows, :] = _layer_norm(alpha * x_ref[rows, :] + z, g_ref[...], b_ref[...])


def _mix_ln(x, oa, ob, p_nat, w_pa, w_pb, w_out, ln_g, ln_b, alpha, tm=512):
    m, dm = x.shape

    def const(arr):
        return pl.BlockSpec(arr.shape, lambda i: (0, 0), pipeline_mode=pl.Buffered(1))

    return pl.pallas_call(
        functools.partial(_mix_ln_kernel, alpha=alpha),
        grid=(m // tm,),
        in_specs=[
            pl.BlockSpec((tm, dm), lambda i: (i, 0)),
            pl.BlockSpec((tm, oa.shape[1]), lambda i: (i, 0)),
            pl.BlockSpec((ob.shape[0], tm), lambda i: (0, i)),
            pl.BlockSpec((tm, dm), lambda i: (i, NAT_GA // dm)),
            pl.BlockSpec((tm, dm), lambda i: (i, NAT_GB // dm)),
            const(w_pa), const(w_pb), const(w_out), const(ln_g), const(ln_b),
        ],
        out_specs=pl.BlockSpec((tm, dm), lambda i: (i, 0)),
        out_shape=jax.ShapeDtypeStruct((m, dm), F32),
        compiler_params=_params(("arbitrary",)),
        name="mix_ln",
    )(x, oa, ob, p_nat, p_nat, w_pa, w_pb, w_out, ln_g, ln_b)


def _ffn_step(x_ref, wg_ref, wu_ref, wd_ref, g_ref, b_ref, o_ref, xb_ref, alpha, first, last):
    sub = x_ref.shape[0] // FFN_SUBTILES
    hs = []
    for k in range(FFN_SUBTILES):
        rows = slice(k * sub, (k + 1) * sub)
        if first:
            xb_ref[rows, :] = x_ref[rows, :].astype(BF16)
        xb = xb_ref[rows, :]
        hs.append((jnp.dot(xb, wg_ref[...], preferred_element_type=F32),
                   jnp.dot(xb, wu_ref[...], preferred_element_type=F32)))
    for k, (gate, up) in enumerate(hs):
        rows = slice(k * sub, (k + 1) * sub)
        act = (gate * jax.nn.sigmoid(gate) * up).astype(BF16)
        base = alpha * x_ref[rows, :] if first else o_ref[rows, :]
        y = base + jnp.dot(act, wd_ref[...], preferred_element_type=F32)
        o_ref[rows, :] = _layer_norm(y, g_ref[...], b_ref[...]) if last else y


def _ffn_ln_kernel(x_ref, wg_ref, wu_ref, wd_ref, g_ref, b_ref, o_ref, xb_ref, *, alpha):
    f = pl.program_id(1)
    nf = pl.num_programs(1)
    step = functools.partial(_ffn_step, x_ref, wg_ref, wu_ref, wd_ref, g_ref, b_ref, o_ref, xb_ref, alpha)
    pl.when(f == 0)(functools.partial(step, True, False))
    pl.when(jnp.logical_and(f > 0, f < nf - 1))(functools.partial(step, False, False))
    pl.when(f == nf - 1)(functools.partial(step, False, True))


def _ffn_ln(x, w_up, w_down, ln_g, ln_b, alpha, tf=512, tm=1024):
    m, dm = x.shape
    nf = w_down.shape[0] // tf
    return pl.pallas_call(
        functools.partial(_ffn_ln_kernel, alpha=alpha),
        grid=(m // tm, nf),
        in_specs=[
            pl.BlockSpec((tm, dm), lambda i, f: (i, 0)),
            pl.BlockSpec((dm, tf), lambda i, f: (0, f)),
            pl.BlockSpec((dm, tf), lambda i, f: (0, nf + f)),
            pl.BlockSpec((tf, dm), lambda i, f: (f, 0)),
            pl.BlockSpec((1, dm), lambda i, f: (0, 0)),
            pl.BlockSpec((1, dm), lambda i, f: (0, 0)),
        ],
        out_specs=pl.BlockSpec((tm, dm), lambda i, f: (i, 0)),
        out_shape=jax.ShapeDtypeStruct((m, dm), F32),
        scratch_shapes=[pltpu.VMEM((tm, dm), BF16)],
        compiler_params=_params(("arbitrary", "arbitrary")),
        name="ffn_ln",
    )(x, w_up, w_up, w_down, ln_g, ln_b)


def _bias_tables(rel_bias):
    def toeplitz(rows, halo, dil, cols):
        width = rows + 2 * halo
        deltas = np.arange(-(halo + rows - 1), rows + halo)
        n = deltas.size
        diag = rel_bias[_t5_bucket(deltas * dil)][:, cols].T.astype(F32)
        skew = jnp.tile(jnp.pad(diag, ((0, 0), (0, 1))), (1, rows))[:, :rows * n].reshape(-1, rows, n)
        return skew[:, :, rows - 1:rows - 1 + width]

    bias_a = []
    for g, (window, dil) in enumerate(A_GROUPS):
        assert window // (2 * dil) == A_HALO
        heads = toeplitz(A_QROWS, A_HALO, dil, slice(g * A_HEADS_PER_GROUP, (g + 1) * A_HEADS_PER_GROUP))
        masks = _band_masks(A_QROWS, A_HALO, A_HALO)
        bias_a.append(jnp.where(masks[:, None], heads[None], NEG_INF))

    heads = toeplitz(B_BLOCK, B_BLOCK, 1, slice(A_HEADS, A_HEADS + B_Q_HEADS))
    masks = _band_masks(B_BLOCK, B_BLOCK, B_HALF)
    bias_b = jnp.where(masks[:, None], heads[None], NEG_INF)
    bias_b = bias_b.reshape(3, B_KV_HEADS, B_REP, B_BLOCK, 3 * B_BLOCK).transpose(0, 1, 4, 2, 3)
    bias_b = bias_b.reshape(3, B_KV_HEADS, 3 * B_BLOCK, B_REP * B_BLOCK)
    return bias_a, bias_b


def _prep_w_in(w_in):
    qa, ka, va = (w_in[:, i * A_QKV_W:(i + 1) * A_QKV_W] for i in range(3))
    o = 3 * A_QKV_W
    qb = w_in[:, o:o + B_Q_W]
    kb = w_in[:, o + B_Q_W:o + B_Q_W + B_KV_W]
    vb = w_in[:, o + B_Q_W + B_KV_W:o + B_Q_W + 2 * B_KV_W]
    o += B_Q_W + 2 * B_KV_W
    dm = w_in.shape[0]
    ga = w_in[:, o:o + dm]
    gb = w_in[:, o + dm:o + 2 * dm]

    def grp(w, g):
        return w[:, g * A_GROUP_W:(g + 1) * A_GROUP_W]

    w_nat = jnp.concatenate([ga, gb, qb, grp(qa, 0), grp(ka, 0), grp(va, 0), kb, vb], axis=1).astype(BF16)
    w_perm = [jnp.concatenate([grp(qa, g), grp(ka, g), grp(va, g)], axis=1).astype(BF16) for g in (1, 2)]
    s_nat = np.ones((1, NAT_W), np.float32)
    s_nat[:, NAT_QB:NAT_QB + B_Q_W] = QK_SCALE
    s_nat[:, NAT_Q0:NAT_Q0 + A_GROUP_W] = QK_SCALE
    return w_nat, w_perm, jnp.asarray(s_nat)


def kernel(x, rel_bias, w_in, sink, w_pa, w_pb, w_out, ln1_g, ln1_b, w_up, w_down, ln2_g, ln2_b):
    batch, seq, dm = x.shape
    depth = w_in.shape[0]
    alpha = (2 * depth) ** 0.25
    bias_a, bias_b = _bias_tables(rel_bias)
    h = x.reshape(batch * seq, dm)
    for l in range(depth):
        w_nat, (w4, w16), s_nat = _prep_w_in(w_in[l])
        sink_row = jnp.repeat(sink[l].reshape(B_KV_HEADS, B_REP), B_BLOCK, axis=1)[:, None, :].astype(F32)
        p_nat, vbt, hb = _proj_nat(h, w_nat, s_nat)
        a4, a16 = _proj_perm(hb, w4, w16, batch, seq)
        oa = _attn_a(p_nat, a4, a16, bias_a, batch, seq)
        ob = _attn_b(p_nat, vbt, bias_b, sink_row, batch, seq)
        h = _mix_ln(h, oa, ob, p_nat, w_pa[l].astype(BF16), w_pb[l].astype(BF16), w_out[l].astype(BF16),
                    ln1_g[l][None], ln1_b[l][None], alpha)
        h = _ffn_ln(h, w_up[l].astype(BF16), w_down[l].astype(BF16), ln2_g[l][None], ln2_b[l][None], alpha)
    return h.reshape(batch, seq, dm)
```

```python
import functools
import math

import numpy as np
import jax
import jax.numpy as jnp
from jax import lax
from jax.experimental import pallas as pl
from jax.experimental.pallas import tpu as pltpu

HEAD_DIM = 128
A_GROUPS = ((128, 1), (512, 4), (2048, 16))
A_HEADS_PER_GROUP = 4
A_HEADS = A_HEADS_PER_GROUP * len(A_GROUPS)
A_HALO = 64
A_GROUP_W = A_HEADS_PER_GROUP * HEAD_DIM
A_QKV_W = A_HEADS * HEAD_DIM
B_Q_HEADS = 8
B_KV_HEADS = 2
B_REP = B_Q_HEADS // B_KV_HEADS
B_HALF = 128
B_BLOCK = 128
B_Q_W = B_Q_HEADS * HEAD_DIM
B_KV_W = B_KV_HEADS * HEAD_DIM
NUM_BUCKETS = 32
MAX_DISTANCE = 1024
LN_EPS = 1e-5
NEG_INF = -1e30
LOG2E = math.log2(math.e)
QK_SCALE = HEAD_DIM ** -0.5 * LOG2E
A_QROWS = 128
A_BATCH = 8
B_UNROLL = 4
B_ONES_ROWS = 16
PROJ_SUBTILES = 4
MIX_SUBTILES = 2
FFN_SUBTILES = 4

BF16 = jnp.bfloat16
F32 = jnp.float32

D_MODEL = 2048
_NAT_WIDTHS = (D_MODEL, D_MODEL, B_Q_W, A_GROUP_W, A_GROUP_W, A_GROUP_W, B_KV_W, B_KV_W)
NAT_GA, NAT_GB, NAT_QB, NAT_Q0, NAT_K0, NAT_V0, NAT_KB, NAT_VB = (
    int(c) for c in np.cumsum((0,) + _NAT_WIDTHS[:-1]))
NAT_W = sum(_NAT_WIDTHS)

VMEM_LIMIT = 56 * 1024 * 1024


def _t5_bucket(rel):
    half = NUM_BUCKETS // 2
    max_exact = half // 2
    n = np.abs(rel)
    scaled = np.log(np.maximum(n, 1) / max_exact) / math.log(MAX_DISTANCE / max_exact)
    large = np.minimum(max_exact + (scaled * (half - max_exact)).astype(np.int64), half - 1)
    return (np.where(rel > 0, half, 0) + np.where(n < max_exact, n, large)).astype(np.int32)


def _band_masks(rows, halo, half):
    kj = np.arange(rows + 2 * halo)[None, :]
    band = np.abs(kj - halo - np.arange(rows)[:, None]) <= half
    return np.stack([band, band & (kj >= halo), band & (kj < rows + halo)])


def _layer_norm(y, g, b):
    mu = jnp.mean(y, axis=-1, keepdims=True)
    yc = y - mu
    var = jnp.mean(yc * yc, axis=-1, keepdims=True)
    return yc * lax.rsqrt(var + LN_EPS) * g + b


def _params(sem):
    return pltpu.CompilerParams(dimension_semantics=sem, vmem_limit_bytes=VMEM_LIMIT)


def _proj_nat_step(x_ref, w_ref, s_ref, o_ref, vt_ref, xb_ref, first, last):
    sub = x_ref.shape[0] // PROJ_SUBTILES
    for k in range(PROJ_SUBTILES):
        rows = slice(k * sub, (k + 1) * sub)
        if first:
            xb_ref[rows, :] = x_ref[rows, :].astype(BF16)
        res = jnp.dot(xb_ref[rows, :], w_ref[...], preferred_element_type=F32) * s_ref[...]
        o_ref[rows, :] = res.astype(o_ref.dtype)
        if last:
            vt_ref[:, rows] = res[:, res.shape[1] - B_KV_W:].T.astype(vt_ref.dtype)


def _proj_nat_kernel(x_ref, w_ref, s_ref, o_ref, vt_ref, xb_ref):
    j = pl.program_id(1)
    nj = pl.num_programs(1)
    step = functools.partial(_proj_nat_step, x_ref, w_ref, s_ref, o_ref, vt_ref, xb_ref)
    pl.when(j == 0)(functools.partial(step, True, False))
    pl.when(jnp.logical_and(j > 0, j < nj - 1))(functools.partial(step, False, False))
    pl.when(j == nj - 1)(functools.partial(step, False, True))


def _proj_nat(x, w, colscale, tm=1024, tn=1792):
    m, k = x.shape
    n = w.shape[1]
    assert m % tm == 0 and n % tn == 0 and n // tn >= 2 and tn >= B_KV_W and n - NAT_VB == B_KV_W
    return pl.pallas_call(
        _proj_nat_kernel,
        grid=(m // tm, n // tn),
        in_specs=[
            pl.BlockSpec((tm, k), lambda i, j: (i, 0)),
            pl.BlockSpec((k, tn), lambda i, j: (0, j)),
            pl.BlockSpec((1, tn), lambda i, j: (0, j)),
        ],
        out_specs=[pl.BlockSpec((tm, tn), lambda i, j: (i, j)),
                   pl.BlockSpec((B_KV_W, tm), lambda i, j: (0, i)),
                   pl.BlockSpec((tm, k), lambda i, j: (i, 0))],
        out_shape=[jax.ShapeDtypeStruct((m, n), BF16), jax.ShapeDtypeStruct((B_KV_W, m), BF16),
                   jax.ShapeDtypeStruct((m, k), BF16)],
        compiler_params=_params(("arbitrary", "arbitrary")),
        name="proj_nat",
    )(x, w, colscale)


def _proj_perm_kernel(xb_ref, w4_ref, w16_ref, o4_ref, o16_ref, res_ref, tmp_ref, *, tm):
    gw = A_GROUP_W
    ncb = gw // HEAD_DIM
    q4 = tm // 4
    for g, (d, w_ref, o_ref) in enumerate(((4, w4_ref, o4_ref), (16, w16_ref, o16_ref))):
        for part in range(3):
            slot = 3 * g + part
            cols = slice(part * gw, (part + 1) * gw)
            res = jnp.dot(xb_ref[...], w_ref[:, cols], preferred_element_type=F32)
            if part == 0:
                res = res * QK_SCALE
            for cb in range(ncb):
                res_ref[slot, cb] = res[:, cb * HEAD_DIM:(cb + 1) * HEAD_DIM]
            for cb in range(ncb):
                ocols = slice(part * gw + cb * HEAD_DIM, part * gw + (cb + 1) * HEAD_DIM)
                if d == 4:
                    for r in range(4):
                        o_ref[0, r, :, ocols] = res_ref[slot, cb, pl.ds(r, q4, stride=4), :].astype(BF16)
                else:
                    for c1 in range(4):
                        tmp_ref[part, cb, c1 * q4:(c1 + 1) * q4, :] = res_ref[slot, cb, pl.ds(c1, q4, stride=4), :]
                    for c1 in range(4):
                        for c2 in range(4):
                            o_ref[0, c1 + 4 * c2, :, ocols] = (
                                tmp_ref[part, cb, pl.ds(c1 * q4 + c2, q4 // 4, stride=4), :].astype(BF16))


def _proj_perm(xb, w4, w16, batch, seq, tm=1024):
    m, k = xb.shape
    n = w4.shape[1]
    nt = seq // tm
    assert seq % tm == 0 and tm % (16 * 16) == 0
    outs = []
    out_specs = []
    for d in (4, 16):
        outs.append(jax.ShapeDtypeStruct((batch, d, seq // d, n), BF16))
        out_specs.append(pl.BlockSpec((1, d, tm // d, n), lambda i: (i // nt, 0, i % nt, 0)))
    return pl.pallas_call(
        functools.partial(_proj_perm_kernel, tm=tm),
        grid=(m // tm,),
        in_specs=[
            pl.BlockSpec((tm, k), lambda i: (i, 0)),
            pl.BlockSpec((k, n), lambda i: (0, 0), pipeline_mode=pl.Buffered(1)),
            pl.BlockSpec((k, n), lambda i: (0, 0), pipeline_mode=pl.Buffered(1)),
        ],
        out_specs=out_specs,
        out_shape=outs,
        scratch_shapes=[pltpu.VMEM((6, A_GROUP_W // HEAD_DIM, tm, HEAD_DIM), F32),
                        pltpu.VMEM((3, A_GROUP_W // HEAD_DIM, tm, HEAD_DIM), F32)],
        compiler_params=_params(("arbitrary",)),
        name="proj_perm",
    )(xb, w4, w16)


def _edge_variant(c, nblk, t, nt):
    first = jnp.logical_and(c == 0, t == 0)
    last = jnp.logical_and(c == nblk - 1, t == nt - 1)
    return jnp.where(first, 1, jnp.where(last, 2, 0))


def _attn_a_kernel(*refs, tq):
    ins = refs[:21]
    bias_refs = refs[21:24]
    o_ref = refs[24]
    og_ref, lse_ref = refs[25:27]
    win_refs = refs[27:33]
    t = pl.program_id(1)
    nt = pl.num_programs(1)
    h = pl.program_id(2)
    halo = A_HALO
    qr = A_QROWS

    for g, (_, d) in enumerate(A_GROUPS):
        q_ref, kp_ref, k_ref, kn_ref, vp_ref, v_ref, vn_ref = ins[7 * g:7 * g + 7]
        kw_ref, vw_ref = win_refs[2 * g:2 * g + 2]
        bias_ref = bias_refs[g]
        n = tq // d
        nblk = n // qr

        for w_ref, p_ref, c_ref, n_ref in ((kw_ref, kp_ref, k_ref, kn_ref), (vw_ref, vp_ref, v_ref, vn_ref)):
            w_ref[:, 0:halo, :] = p_ref[...].reshape(d, halo, HEAD_DIM)
            w_ref[:, halo:halo + n, :] = c_ref[...].reshape(d, n, HEAD_DIM)
            w_ref[:, halo + n:, :] = n_ref[...].reshape(d, halo, HEAD_DIM)

        units = [(r, c) for r in range(d) for c in range(nblk)]
        for u0 in range(0, len(units), A_BATCH):
            qs, kws, vws, biases, dsts = [], [], [], [], []
            for r, c in units[u0:u0 + A_BATCH]:
                qrows = slice(c * qr, (c + 1) * qr)
                wrows = slice(c * qr, (c + 1) * qr + 2 * halo)
                qs.append(q_ref[qrows, :] if g == 0 else q_ref[0, r, qrows, :])
                kws.append(kw_ref[r, wrows, :])
                vws.append(vw_ref[r, wrows, :])
                biases.append(bias_ref[_edge_variant(c, nblk, t, nt), h])
                dsts.append(qrows if d == 1 else pl.ds(r + d * qr * c, qr, stride=d))
            q3 = jnp.stack(qs)
            kw3 = jnp.stack(kws)
            vw3 = jnp.stack(vws)
            s = jnp.einsum("uqd,ukd->uqk", q3, kw3, preferred_element_type=F32) + jnp.stack(biases)
            m = jnp.max(s, axis=-1, keepdims=True)
            p = jnp.exp2(s - m).astype(BF16)
            vx = jnp.concatenate([vw3, jnp.ones_like(vw3)], axis=-1)
            acc = jnp.einsum("uqk,ukd->uqd", p, vx, preferred_element_type=F32)
            den = acc[..., HEAD_DIM:]
            og = acc[..., :HEAD_DIM] / den
            lse = m + jnp.log2(den)
            for u, dst in enumerate(dsts):
                og_ref[g, dst, :] = og[u]
                lse_ref[g, dst, :] = lse[u]

    chunk = 256

    def merge(i, carry):
        rs = pl.ds(pl.multiple_of(i * chunk, chunk), chunk)
        ls = [lse_ref[g, rs, :] for g in range(3)]
        mx = jnp.maximum(jnp.maximum(ls[0], ls[1]), ls[2])
        num = jnp.zeros((chunk, HEAD_DIM), F32)
        dn = jnp.zeros((chunk, HEAD_DIM), F32)
        for g in range(3):
            e = jnp.exp2(ls[g] - mx)
            num = num + e * og_ref[g, rs, :]
            dn = dn + e
        o_ref[rs, :] = (num / dn).astype(o_ref.dtype)
        return carry

    lax.fori_loop(0, tq // chunk, merge, 0)


def _attn_a(p_nat, a4, a16, bias, batch, seq, tq=4096):
    m = p_nat.shape[0]
    nt = seq // tq
    assert seq % tq == 0 and nt >= 2 and tq % (A_GROUPS[-1][1] * A_QROWS) == 0
    halo = A_HALO
    hd = HEAD_DIM
    hg = A_HEADS_PER_GROUP
    in_specs = []
    args = []

    nb_tile = tq // halo
    last_halo = m // halo - 1

    def g0_main(col0):
        return pl.BlockSpec((tq, hd), lambda b, t, h: (b * nt + t, col0 + h))

    def g0_prev(col0):
        return pl.BlockSpec((halo, hd), lambda b, t, h: (jnp.maximum((b * nt + t) * nb_tile - 1, 0), col0 + h))

    def g0_next(col0):
        return pl.BlockSpec((halo, hd),
                            lambda b, t, h: (jnp.minimum((b * nt + t + 1) * nb_tile, last_halo), col0 + h))

    q0, k0, v0 = NAT_Q0 // hd, NAT_K0 // hd, NAT_V0 // hd
    in_specs += [g0_main(q0), g0_prev(k0), g0_main(k0), g0_next(k0), g0_prev(v0), g0_main(v0), g0_next(v0)]
    args += [p_nat] * 7

    for arr, d in ((a4, 4), (a16, 16)):
        n = tq // d
        nb = n // halo
        lastb = seq // d // halo - 1

        def main(part, d=d, n=n):
            return pl.BlockSpec((1, d, n, hd), lambda b, t, h: (b, 0, t, part * hg + h))

        def prev(part, d=d, nb=nb):
            return pl.BlockSpec((1, d, halo, hd),
                                lambda b, t, h: (b, 0, jnp.maximum(t * nb - 1, 0), part * hg + h))

        def nxt(part, d=d, nb=nb, lastb=lastb):
            return pl.BlockSpec((1, d, halo, hd),
                                lambda b, t, h: (b, 0, jnp.minimum((t + 1) * nb, lastb), part * hg + h))

        in_specs += [main(0), prev(1), main(1), nxt(1), prev(2), main(2), nxt(2)]
        args += [arr] * 7

    for tab in bias:
        in_specs.append(pl.BlockSpec(tab.shape, lambda b, t, h: (0, 0, 0, 0), pipeline_mode=pl.Buffered(1)))
        args.append(tab)

    return pl.pallas_call(
        functools.partial(_attn_a_kernel, tq=tq),
        grid=(batch, nt, hg),
        in_specs=in_specs,
        out_specs=pl.BlockSpec((tq, hd), lambda b, t, h: (b * nt + t, h)),
        out_shape=jax.ShapeDtypeStruct((m, A_GROUP_W), BF16),
        scratch_shapes=(
            [pltpu.VMEM((3, tq, hd), F32)] * 2
            + [pltpu.VMEM((d, tq // d + 2 * halo, hd), BF16) for _, d in A_GROUPS for _ in range(2)]),
        compiler_params=_params(("arbitrary", "arbitrary", "arbitrary")),
        name="attn_a",
    )(*args)


def _attn_b_kernel(q_ref, kp_ref, k_ref, kn_ref, vp_ref, v_ref, vn_ref, bias_ref, sink_ref, o_ref,
                   kw_ref, vw_ref, *, tq):
    t = pl.program_id(1)
    nt = pl.num_programs(1)
    blk = B_BLOCK
    nblk = tq // blk

    kw_ref[0:blk, :] = kp_ref[...]
    kw_ref[blk:blk + tq, :] = k_ref[...]
    kw_ref[blk + tq:, :] = kn_ref[...]
    vw_ref[:, 0:blk] = vp_ref[...]
    vw_ref[:, blk:blk + tq] = v_ref[...]
    vw_ref[:, blk + tq:] = vn_ref[...]
    ones = jnp.ones((B_ONES_ROWS, 3 * blk), BF16)

    def block(it, carry):
        units = []
        for j in range(B_UNROLL):
            c = it * B_UNROLL + j
            variant = _edge_variant(c, nblk, t, nt)
            rows = pl.ds(pl.multiple_of(c * blk, blk), blk)
            wrows = pl.ds(pl.multiple_of(c * blk, blk), 3 * blk)
            for kvh in range(B_KV_HEADS):
                units.append((kvh, variant, rows, wrows))
        scores = []
        for kvh, variant, rows, wrows in units:
            kcols = slice(kvh * HEAD_DIM, (kvh + 1) * HEAD_DIM)
            q4 = jnp.concatenate(
                [q_ref[rows, (kvh * B_REP + rep) * HEAD_DIM:(kvh * B_REP + rep + 1) * HEAD_DIM]
                 for rep in range(B_REP)], axis=0)
            scores.append(lax.dot_general(kw_ref[wrows, kcols], q4, (((1,), (1,)), ((), ())),
                                          preferred_element_type=F32))
        probs = []
        for (kvh, variant, rows, wrows), s in zip(units, scores):
            s = s + bias_ref[variant, kvh]
            sink = sink_ref[kvh]
            m = jnp.maximum(jnp.max(s, axis=0, keepdims=True), sink)
            probs.append((jnp.exp2(s - m).astype(BF16), jnp.exp2(sink - m)))
        for (kvh, variant, rows, wrows), (p, psink) in zip(units, probs):
            kcols = slice(kvh * HEAD_DIM, (kvh + 1) * HEAD_DIM)
            vx = jnp.concatenate([vw_ref[kcols, wrows], ones], axis=0)
            acc = jnp.dot(vx, p, preferred_element_type=F32)
            den = acc[HEAD_DIM:HEAD_DIM + 1, :] + psink
            o = (acc[:HEAD_DIM, :] / den).astype(o_ref.dtype)
            for rep in range(B_REP):
                h = kvh * B_REP + rep
                o_ref[h * HEAD_DIM:(h + 1) * HEAD_DIM, rows] = o[:, rep * blk:(rep + 1) * blk]
        return carry

    lax.fori_loop(0, nblk // B_UNROLL, block, 0)


def _attn_b(p_nat, vbt, bias, sink, batch, seq, tq=2048):
    m = p_nat.shape[0]
    nt = seq // tq
    assert seq % tq == 0 and nt >= 2 and tq % (B_UNROLL * B_BLOCK) == 0
    blk = B_BLOCK
    nb_tile = tq // blk
    lastb = m // blk - 1
    q_col = NAT_QB // B_Q_W
    k_col = NAT_KB // B_KV_W

    def first(b, t):
        return jnp.maximum((b * nt + t) * nb_tile - 1, 0)

    def after(b, t):
        return jnp.minimum((b * nt + t + 1) * nb_tile, lastb)

    return pl.pallas_call(
        functools.partial(_attn_b_kernel, tq=tq),
        grid=(batch, nt),
        in_specs=[
            pl.BlockSpec((tq, B_Q_W), lambda b, t: (b * nt + t, q_col)),
            pl.BlockSpec((blk, B_KV_W), lambda b, t: (first(b, t), k_col)),
            pl.BlockSpec((tq, B_KV_W), lambda b, t: (b * nt + t, k_col)),
            pl.BlockSpec((blk, B_KV_W), lambda b, t: (after(b, t), k_col)),
            pl.BlockSpec((B_KV_W, blk), lambda b, t: (0, first(b, t))),
            pl.BlockSpec((B_KV_W, tq), lambda b, t: (0, b * nt + t)),
            pl.BlockSpec((B_KV_W, blk), lambda b, t: (0, after(b, t))),
            pl.BlockSpec(bias.shape, lambda b, t: (0, 0, 0, 0)),
            pl.BlockSpec(sink.shape, lambda b, t: (0, 0, 0)),
        ],
        out_specs=pl.BlockSpec((B_Q_W, tq), lambda b, t: (0, b * nt + t)),
        out_shape=jax.ShapeDtypeStruct((B_Q_W, m), BF16),
        scratch_shapes=[pltpu.VMEM((tq + 2 * blk, B_KV_W), BF16), pltpu.VMEM((B_KV_W, tq + 2 * blk), BF16)],
        compiler_params=_params(("arbitrary", "arbitrary")),
        name="attn_b",
    )(p_nat, p_nat, p_nat, p_nat, vbt, vbt, vbt, bias, sink)


def _mix_ln_kernel(x_ref, oa_ref, ob_ref, ga_ref, gb_ref, wpa_ref, wpb_ref, wo_ref, g_ref, b_ref, o_ref, *, alpha):
    sub = x_ref.shape[0] // MIX_SUBTILES
    ys = []
    for k in range(MIX_SUBTILES):
        rows = slice(k * sub, (k + 1) * sub)
        ya = jnp.dot(oa_ref[rows, :], wpa_ref[...], preferred_element_type=F32)
        yb = lax.dot_general(ob_ref[:, rows], wpb_ref[...], (((0,), (0,)), ((), ())),
                             preferred_element_type=F32)
        ys.append((ya, yb))
    for k, (ya, yb) in enumerate(ys):
        rows = slice(k * sub, (k + 1) * sub)
        merged = (jax.nn.sigmoid(ga_ref[rows, :].astype(F32)) * ya
                  + jax.nn.sigmoid(gb_ref[rows, :].astype(F32)) * yb)
        z = jnp.dot(merged.astype(BF16), wo_ref[...], preferred_element_type=F32)
        o_ref[rows, :] = _layer_norm(alpha * x_ref[rows, :] + z, g_ref[...], b_ref[...])


def _mix_ln(x, oa, ob, p_nat, w_pa, w_pb, w_out, ln_g, ln_b, alpha, tm=512):
    m, dm = x.shape

    def const(arr):
        return pl.BlockSpec(arr.shape, lambda i: (0, 0), pipeline_mode=pl.Buffered(1))

    return pl.pallas_call(
        functools.partial(_mix_ln_kernel, alpha=alpha),
        grid=(m // tm,),
        in_specs=[
            pl.BlockSpec((tm, dm), lambda i: (i, 0)),
            pl.BlockSpec((tm, oa.shape[1]), lambda i: (i, 0)),
            pl.BlockSpec((ob.shape[0], tm), lambda i: (0, i)),
            pl.BlockSpec((tm, dm), lambda i: (i, NAT_GA // dm)),
            pl.BlockSpec((tm, dm), lambda i: (i, NAT_GB // dm)),
            const(w_pa), const(w_pb), const(w_out), const(ln_g), const(ln_b),
        ],
        out_specs=pl.BlockSpec((tm, dm), lambda i: (i, 0)),
        out_shape=jax.ShapeDtypeStruct((m, dm), F32),
        compiler_params=_params(("arbitrary",)),
        name="mix_ln",
    )(x, oa, ob, p_nat, p_nat, w_pa, w_pb, w_out, ln_g, ln_b)


def _ffn_step(x_ref, wg_ref, wu_ref, wd_ref, g_ref, b_ref, o_ref, xb_ref, alpha, first, last):
    sub = x_ref.shape[0] // FFN_SUBTILES
    hs = []
    for k in range(FFN_SUBTILES):
        rows = slice(k * sub, (k + 1) * sub)
        if first:
            xb_ref[rows, :] = x_ref[rows, :].astype(BF16)
        xb = xb_ref[rows, :]
        hs.append((jnp.dot(xb, wg_ref[...], preferred_element_type=F32),
                   jnp.dot(xb, wu_ref[...], preferred_element_type=F32)))
    for k, (gate, up) in enumerate(hs):
        rows = slice(k * sub, (k + 1) * sub)
        act = (gate * jax.nn.sigmoid(gate) * up).astype(BF16)
        base = alpha * x_ref[rows, :] if first else o_ref[rows, :]
        y = base + jnp.dot(act, wd_ref[...], preferred_element_type=F32)
        o_ref[rows, :] = _layer_norm(y, g_ref[...], b_ref[...]) if last else y


def _ffn_ln_kernel(x_ref, wg_ref, wu_ref, wd_ref, g_ref, b_ref, o_ref, xb_ref, *, alpha):
    f = pl.program_id(1)
    nf = pl.num_programs(1)
    step = functools.partial(_ffn_step, x_ref, wg_ref, wu_ref, wd_ref, g_ref, b_ref, o_ref, xb_ref, alpha)
    pl.when(f == 0)(functools.partial(step, True, False))
    pl.when(jnp.logical_and(f > 0, f < nf - 1))(functools.partial(step, False, False))
    pl.when(f == nf - 1)(functools.partial(step, False, True))


def _ffn_ln(x, w_up, w_down, ln_g, ln_b, alpha, tf=512, tm=1024):
    m, dm = x.shape
    nf = w_down.shape[0] // tf
    assert m % tm == 0 and w_down.shape[0] % tf == 0 and nf >= 2 and w_up.shape[1] == 2 * w_down.shape[0]
    return pl.pallas_call(
        functools.partial(_ffn_ln_kernel, alpha=alpha),
        grid=(m // tm, nf),
        in_specs=[
            pl.BlockSpec((tm, dm), lambda i, f: (i, 0)),
            pl.BlockSpec((dm, tf), lambda i, f: (0, f)),
            pl.BlockSpec((dm, tf), lambda i, f: (0, nf + f)),
            pl.BlockSpec((tf, dm), lambda i, f: (f, 0)),
            pl.BlockSpec((1, dm), lambda i, f: (0, 0)),
            pl.BlockSpec((1, dm), lambda i, f: (0, 0)),
        ],
        out_specs=pl.BlockSpec((tm, dm), lambda i, f: (i, 0)),
        out_shape=jax.ShapeDtypeStruct((m, dm), F32),
        scratch_shapes=[pltpu.VMEM((tm, dm), BF16)],
        compiler_params=_params(("arbitrary", "arbitrary")),
        name="ffn_ln",
    )(x, w_up, w_up, w_down, ln_g, ln_b)


def _bias_tables(rel_bias):
    def toeplitz(rows, halo, dil, cols):
        width = rows + 2 * halo
        deltas = np.arange(-(halo + rows - 1), rows + halo)
        n = deltas.size
        diag = rel_bias[_t5_bucket(deltas * dil)][:, cols].T.astype(F32) * LOG2E
        skew = jnp.tile(jnp.pad(diag, ((0, 0), (0, 1))), (1, rows))[:, :rows * n].reshape(-1, rows, n)
        return skew[:, :, rows - 1:rows - 1 + width]

    bias_a = []
    for g, (window, dil) in enumerate(A_GROUPS):
        assert window // (2 * dil) == A_HALO
        heads = toeplitz(A_QROWS, A_HALO, dil, slice(g * A_HEADS_PER_GROUP, (g + 1) * A_HEADS_PER_GROUP))
        masks = _band_masks(A_QROWS, A_HALO, A_HALO)
        bias_a.append(jnp.where(masks[:, None], heads[None], NEG_INF))

    heads = toeplitz(B_BLOCK, B_BLOCK, 1, slice(A_HEADS, A_HEADS + B_Q_HEADS))
    masks = _band_masks(B_BLOCK, B_BLOCK, B_HALF)
    bias_b = jnp.where(masks[:, None], heads[None], NEG_INF)
    bias_b = bias_b.reshape(3, B_KV_HEADS, B_REP, B_BLOCK, 3 * B_BLOCK).transpose(0, 1, 4, 2, 3)
    bias_b = bias_b.reshape(3, B_KV_HEADS, 3 * B_BLOCK, B_REP * B_BLOCK)
    return bias_a, bias_b


def _prep_w_in(w_in):
    qa, ka, va = (w_in[:, i * A_QKV_W:(i + 1) * A_QKV_W] for i in range(3))
    o = 3 * A_QKV_W
    qb = w_in[:, o:o + B_Q_W]
    kb = w_in[:, o + B_Q_W:o + B_Q_W + B_KV_W]
    vb = w_in[:, o + B_Q_W + B_KV_W:o + B_Q_W + 2 * B_KV_W]
    o += B_Q_W + 2 * B_KV_W
    dm = w_in.shape[0]
    ga = w_in[:, o:o + dm]
    gb = w_in[:, o + dm:o + 2 * dm]

    def grp(w, g):
        return w[:, g * A_GROUP_W:(g + 1) * A_GROUP_W]

    w_nat = jnp.concatenate([ga, gb, qb, grp(qa, 0), grp(ka, 0), grp(va, 0), kb, vb], axis=1).astype(BF16)
    w_perm = [jnp.concatenate([grp(qa, g), grp(ka, g), grp(va, g)], axis=1).astype(BF16) for g in (1, 2)]
    s_nat = np.ones((1, NAT_W), np.float32)
    s_nat[:, NAT_QB:NAT_QB + B_Q_W] = QK_SCALE
    s_nat[:, NAT_Q0:NAT_Q0 + A_GROUP_W] = QK_SCALE
    return w_nat, w_perm, jnp.asarray(s_nat)


def kernel(x, rel_bias, w_in, sink, w_pa, w_pb, w_out, ln1_g, ln1_b, w_up, w_down, ln2_g, ln2_b):
    batch, seq, dm = x.shape
    depth = w_in.shape[0]
    assert dm == D_MODEL and w_in.shape[1:] == (dm, 3 * A_QKV_W + B_Q_W + 2 * B_KV_W + 2 * dm)
    assert rel_bias.shape == (NUM_BUCKETS, A_HEADS + B_Q_HEADS)
    alpha = (2 * depth) ** 0.25
    bias_a, bias_b = _bias_tables(rel_bias)
    h = x.reshape(batch * seq, dm)
    for l in range(depth):
        w_nat, (w4, w16), s_nat = _prep_w_in(w_in[l])
        sink_row = jnp.repeat(sink[l].reshape(B_KV_HEADS, B_REP), B_BLOCK, axis=1)[:, None, :].astype(F32) * LOG2E
        p_nat, vbt, hb = _proj_nat(h, w_nat, s_nat)
        a4, a16 = _proj_perm(hb, w4, w16, batch, seq)
        oa = _attn_a(p_nat, a4, a16, bias_a, batch, seq)
        ob = _attn_b(p_nat, vbt, bias_b, sink_row, batch, seq)
        h = _mix_ln(h, oa, ob, p_nat, w_pa[l].astype(BF16), w_pb[l].astype(BF16), w_out[l].astype(BF16),
                    ln1_g[l][None], ln1_b[l][None], alpha)
        h = _ffn_ln(h, w_up[l].astype(BF16), w_down[l].astype(BF16), ln2_g[l][None], ln2_b[l][None], alpha)
    return h.reshape(batch, seq, dm)
```

```python
import functools
import math

import numpy as np
import jax
import jax.numpy as jnp
from jax import lax
from jax.experimental import pallas as pl
from jax.experimental.pallas import tpu as pltpu

HEAD_DIM = 128
A_GROUPS = ((128, 1), (512, 4), (2048, 16))
A_HEADS_PER_GROUP = 4
A_HEADS = A_HEADS_PER_GROUP * len(A_GROUPS)
A_HALO = 64
A_GROUP_W = A_HEADS_PER_GROUP * HEAD_DIM
A_QKV_W = A_HEADS * HEAD_DIM
B_Q_HEADS = 8
B_KV_HEADS = 2
B_REP = B_Q_HEADS // B_KV_HEADS
B_HALF = 128
B_BLOCK = 128
B_Q_W = B_Q_HEADS * HEAD_DIM
B_KV_W = B_KV_HEADS * HEAD_DIM
NUM_BUCKETS = 32
MAX_DISTANCE = 1024
LN_EPS = 1e-5
NEG_INF = -1e30
LOG2E = math.log2(math.e)
QK_SCALE = HEAD_DIM ** -0.5 * LOG2E
A_QROWS = 128
A_BATCH = 8
B_UNROLL = 4
B_ONES_ROWS = 16
PROJ_SUBTILES = 4
MIX_SUBTILES = 2
FFN_SUBTILES = 4

BF16 = jnp.bfloat16
F32 = jnp.float32

D_MODEL = 2048
_NAT_WIDTHS = (D_MODEL, D_MODEL, B_Q_W, A_GROUP_W, A_GROUP_W, A_GROUP_W, B_KV_W, B_KV_W)
NAT_GA, NAT_GB, NAT_QB, NAT_Q0, NAT_K0, NAT_V0, NAT_KB, NAT_VB = (
    int(c) for c in np.cumsum((0,) + _NAT_WIDTHS[:-1]))
NAT_W = sum(_NAT_WIDTHS)

VMEM_LIMIT = 56 * 1024 * 1024


def _t5_bucket(rel):
    half = NUM_BUCKETS // 2
    max_exact = half // 2
    n = np.abs(rel)
    scaled = np.log(np.maximum(n, 1) / max_exact) / math.log(MAX_DISTANCE / max_exact)
    large = np.minimum(max_exact + (scaled * (half - max_exact)).astype(np.int64), half - 1)
    return (np.where(rel > 0, half, 0) + np.where(n < max_exact, n, large)).astype(np.int32)


def _band_masks(rows, halo, half):
    kj = np.arange(rows + 2 * halo)[None, :]
    band = np.abs(kj - halo - np.arange(rows)[:, None]) <= half
    return np.stack([band, band & (kj >= halo), band & (kj < rows + halo)])


def _layer_norm(y, g, b):
    mu = jnp.mean(y, axis=-1, keepdims=True)
    yc = y - mu
    var = jnp.mean(yc * yc, axis=-1, keepdims=True)
    return yc * lax.rsqrt(var + LN_EPS) * g + b


def _params(sem):
    return pltpu.CompilerParams(dimension_semantics=sem, vmem_limit_bytes=VMEM_LIMIT)


def _proj_nat_step(x_ref, w_ref, s_ref, o_ref, vt_ref, xb_ref, first, last):
    sub = x_ref.shape[0] // PROJ_SUBTILES
    for k in range(PROJ_SUBTILES):
        rows = slice(k * sub, (k + 1) * sub)
        if first:
            xb_ref[rows, :] = x_ref[rows, :].astype(BF16)
        res = jnp.dot(xb_ref[rows, :], w_ref[...], preferred_element_type=F32) * s_ref[...]
        o_ref[rows, :] = res.astype(o_ref.dtype)
        if last:
            vt_ref[:, rows] = res[:, res.shape[1] - B_KV_W:].T.astype(vt_ref.dtype)


def _proj_nat_kernel(x_ref, w_ref, s_ref, o_ref, vt_ref, xb_ref):
    j = pl.program_id(1)
    nj = pl.num_programs(1)
    step = functools.partial(_proj_nat_step, x_ref, w_ref, s_ref, o_ref, vt_ref, xb_ref)
    pl.when(j == 0)(functools.partial(step, True, False))
    pl.when(jnp.logical_and(j > 0, j < nj - 1))(functools.partial(step, False, False))
    pl.when(j == nj - 1)(functools.partial(step, False, True))


def _proj_nat(x, w, colscale, tm=1024, tn=1792):
    m, k = x.shape
    n = w.shape[1]
    assert m % tm == 0 and n % tn == 0 and n // tn >= 2 and tn >= B_KV_W and n - NAT_VB == B_KV_W
    return pl.pallas_call(
        _proj_nat_kernel,
        grid=(m // tm, n // tn),
        in_specs=[
            pl.BlockSpec((tm, k), lambda i, j: (i, 0)),
            pl.BlockSpec((k, tn), lambda i, j: (0, j)),
            pl.BlockSpec((1, tn), lambda i, j: (0, j)),
        ],
        out_specs=[pl.BlockSpec((tm, tn), lambda i, j: (i, j)),
                   pl.BlockSpec((B_KV_W, tm), lambda i, j: (0, i)),
                   pl.BlockSpec((tm, k), lambda i, j: (i, 0))],
        out_shape=[jax.ShapeDtypeStruct((m, n), BF16), jax.ShapeDtypeStruct((B_KV_W, m), BF16),
                   jax.ShapeDtypeStruct((m, k), BF16)],
        compiler_params=_params(("arbitrary", "arbitrary")),
        name="proj_nat",
    )(x, w, colscale)


def _proj_perm_kernel(xb_ref, w4_ref, w16_ref, o4_ref, o16_ref, res_ref, tmp_ref, *, tm):
    gw = A_GROUP_W
    ncb = gw // HEAD_DIM
    q4 = tm // 4
    for g, (d, w_ref, o_ref) in enumerate(((4, w4_ref, o4_ref), (16, w16_ref, o16_ref))):
        for part in range(3):
            slot = 3 * g + part
            cols = slice(part * gw, (part + 1) * gw)
            res = jnp.dot(xb_ref[...], w_ref[:, cols], preferred_element_type=F32)
            if part == 0:
                res = res * QK_SCALE
            for cb in range(ncb):
                res_ref[slot, cb] = res[:, cb * HEAD_DIM:(cb + 1) * HEAD_DIM]
            for cb in range(ncb):
                ocols = slice(part * gw + cb * HEAD_DIM, part * gw + (cb + 1) * HEAD_DIM)
                if d == 4:
                    for r in range(4):
                        o_ref[0, r, :, ocols] = res_ref[slot, cb, pl.ds(r, q4, stride=4), :].astype(BF16)
                else:
                    for c1 in range(4):
                        tmp_ref[part, cb, c1 * q4:(c1 + 1) * q4, :] = res_ref[slot, cb, pl.ds(c1, q4, stride=4), :]
                    for c1 in range(4):
                        for c2 in range(4):
                            o_ref[0, c1 + 4 * c2, :, ocols] = (
                                tmp_ref[part, cb, pl.ds(c1 * q4 + c2, q4 // 4, stride=4), :].astype(BF16))


def _proj_perm(xb, w4, w16, batch, seq, tm=1024):
    m, k = xb.shape
    n = w4.shape[1]
    nt = seq // tm
    assert seq % tm == 0 and tm % (16 * 16) == 0
    outs = []
    out_specs = []
    for d in (4, 16):
        outs.append(jax.ShapeDtypeStruct((batch, d, seq // d, n), BF16))
        out_specs.append(pl.BlockSpec((1, d, tm // d, n), lambda i: (i // nt, 0, i % nt, 0)))
    return pl.pallas_call(
        functools.partial(_proj_perm_kernel, tm=tm),
        grid=(m // tm,),
        in_specs=[
            pl.BlockSpec((tm, k), lambda i: (i, 0)),
            pl.BlockSpec((k, n), lambda i: (0, 0), pipeline_mode=pl.Buffered(1)),
            pl.BlockSpec((k, n), lambda i: (0, 0), pipeline_mode=pl.Buffered(1)),
        ],
        out_specs=out_specs,
        out_shape=outs,
        scratch_shapes=[pltpu.VMEM((6, A_GROUP_W // HEAD_DIM, tm, HEAD_DIM), F32),
                        pltpu.VMEM((3, A_GROUP_W // HEAD_DIM, tm, HEAD_DIM), F32)],
        compiler_params=_params(("arbitrary",)),
        name="proj_perm",
    )(xb, w4, w16)


def _edge_variant(c, nblk, t, nt):
    first = jnp.logical_and(c == 0, t == 0)
    last = jnp.logical_and(c == nblk - 1, t == nt - 1)
    return jnp.where(first, 1, jnp.where(last, 2, 0))


def _attn_a_kernel(*refs, tq):
    ins = refs[:21]
    bias_refs = refs[21:24]
    o_ref = refs[24]
    og_ref, lse_ref = refs[25:27]
    win_refs = refs[27:33]
    t = pl.program_id(1)
    nt = pl.num_programs(1)
    h = pl.program_id(2)
    halo = A_HALO
    qr = A_QROWS

    for g, (_, d) in enumerate(A_GROUPS):
        q_ref, kp_ref, k_ref, kn_ref, vp_ref, v_ref, vn_ref = ins[7 * g:7 * g + 7]
        kw_ref, vw_ref = win_refs[2 * g:2 * g + 2]
        bias_ref = bias_refs[g]
        n = tq // d
        nblk = n // qr

        for w_ref, p_ref, c_ref, n_ref in ((kw_ref, kp_ref, k_ref, kn_ref), (vw_ref, vp_ref, v_ref, vn_ref)):
            w_ref[:, 0:halo, :] = p_ref[...].reshape(d, halo, HEAD_DIM)
            w_ref[:, halo:halo + n, :] = c_ref[...].reshape(d, n, HEAD_DIM)
            w_ref[:, halo + n:, :] = n_ref[...].reshape(d, halo, HEAD_DIM)

        units = [(r, c) for r in range(d) for c in range(nblk)]
        for u0 in range(0, len(units), A_BATCH):
            qs, kws, vws, biases, dsts = [], [], [], [], []
            for r, c in units[u0:u0 + A_BATCH]:
                qrows = slice(c * qr, (c + 1) * qr)
                wrows = slice(c * qr, (c + 1) * qr + 2 * halo)
                qs.append(q_ref[qrows, :] if g == 0 else q_ref[0, r, qrows, :])
                kws.append(kw_ref[r, wrows, :])
                vws.append(vw_ref[r, wrows, :])
                biases.append(bias_ref[_edge_variant(c, nblk, t, nt), h])
                dsts.append(qrows if d == 1 else pl.ds(r + d * qr * c, qr, stride=d))
            q3 = jnp.stack(qs)
            kw3 = jnp.stack(kws)
            vw3 = jnp.stack(vws)
            s = jnp.einsum("uqd,ukd->uqk", q3, kw3, preferred_element_type=F32) + jnp.stack(biases)
            m = jnp.max(s, axis=-1, keepdims=True)
            p = jnp.exp2(s - m).astype(BF16)
            vx = jnp.concatenate([vw3, jnp.ones_like(vw3)], axis=-1)
            acc = jnp.einsum("uqk,ukd->uqd", p, vx, preferred_element_type=F32)
            den = acc[..., HEAD_DIM:]
            og = acc[..., :HEAD_DIM] / den
            lse = m + jnp.log2(den)
            for u, dst in enumerate(dsts):
                og_ref[g, dst, :] = og[u]
                lse_ref[g, dst, :] = lse[u]

    chunk = 256

    def merge(i, carry):
        rs = pl.ds(pl.multiple_of(i * chunk, chunk), chunk)
        ls = [lse_ref[g, rs, :] for g in range(3)]
        mx = jnp.maximum(jnp.maximum(ls[0], ls[1]), ls[2])
        num = jnp.zeros((chunk, HEAD_DIM), F32)
        dn = jnp.zeros((chunk, HEAD_DIM), F32)
        for g in range(3):
            e = jnp.exp2(ls[g] - mx)
            num = num + e * og_ref[g, rs, :]
            dn = dn + e
        o_ref[rs, :] = (num / dn).astype(o_ref.dtype)
        return carry

    lax.fori_loop(0, tq // chunk, merge, 0)


def _attn_a(p_nat, a4, a16, bias, batch, seq, tq=4096):
    m = p_nat.shape[0]
    nt = seq // tq
    assert seq % tq == 0 and nt >= 2 and tq % (A_GROUPS[-1][1] * A_QROWS) == 0
    halo = A_HALO
    hd = HEAD_DIM
    hg = A_HEADS_PER_GROUP
    in_specs = []
    args = []

    nb_tile = tq // halo
    last_halo = m // halo - 1

    def g0_main(col0):
        return pl.BlockSpec((tq, hd), lambda b, t, h: (b * nt + t, col0 + h))

    def g0_prev(col0):
        return pl.BlockSpec((halo, hd), lambda b, t, h: (jnp.maximum((b * nt + t) * nb_tile - 1, 0), col0 + h))

    def g0_next(col0):
        return pl.BlockSpec((halo, hd),
                            lambda b, t, h: (jnp.minimum((b * nt + t + 1) * nb_tile, last_halo), col0 + h))

    q0, k0, v0 = NAT_Q0 // hd, NAT_K0 // hd, NAT_V0 // hd
    in_specs += [g0_main(q0), g0_prev(k0), g0_main(k0), g0_next(k0), g0_prev(v0), g0_main(v0), g0_next(v0)]
    args += [p_nat] * 7

    for arr, d in ((a4, 4), (a16, 16)):
        n = tq // d
        nb = n // halo
        lastb = seq // d // halo - 1

        def main(part, d=d, n=n):
            return pl.BlockSpec((1, d, n, hd), lambda b, t, h: (b, 0, t, part * hg + h))

        def prev(part, d=d, nb=nb):
            return pl.BlockSpec((1, d, halo, hd),
                                lambda b, t, h: (b, 0, jnp.maximum(t * nb - 1, 0), part * hg + h))

        def nxt(part, d=d, nb=nb, lastb=lastb):
            return pl.BlockSpec((1, d, halo, hd),
                                lambda b, t, h: (b, 0, jnp.minimum((t + 1) * nb, lastb), part * hg + h))

        in_specs += [main(0), prev(1), main(1), nxt(1), prev(2), main(2), nxt(2)]
        args += [arr] * 7

    for tab in bias:
        in_specs.append(pl.BlockSpec(tab.shape, lambda b, t, h: (0, 0, 0, 0), pipeline_mode=pl.Buffered(1)))
        args.append(tab)

    return pl.pallas_call(
        functools.partial(_attn_a_kernel, tq=tq),
        grid=(batch, nt, hg),
        in_specs=in_specs,
        out_specs=pl.BlockSpec((tq, hd), lambda b, t, h: (b * nt + t, h)),
        out_shape=jax.ShapeDtypeStruct((m, A_GROUP_W), BF16),
        scratch_shapes=(
            [pltpu.VMEM((3, tq, hd), F32)] * 2
            + [pltpu.VMEM((d, tq // d + 2 * halo, hd), BF16) for _, d in A_GROUPS for _ in range(2)]),
        compiler_params=_params(("arbitrary", "arbitrary", "arbitrary")),
        name="attn_a",
    )(*args)


def _attn_b_kernel(q_ref, kp_ref, k_ref, kn_ref, vp_ref, v_ref, vn_ref, bias_ref, sink_ref, o_ref,
                   kw_ref, vw_ref, *, tq):
    t = pl.program_id(1)
    nt = pl.num_programs(1)
    blk = B_BLOCK
    nblk = tq // blk

    kw_ref[0:blk, :] = kp_ref[...]
    kw_ref[blk:blk + tq, :] = k_ref[...]
    kw_ref[blk + tq:, :] = kn_ref[...]
    vw_ref[:, 0:blk] = vp_ref[...]
    vw_ref[:, blk:blk + tq] = v_ref[...]
    vw_ref[:, blk + tq:] = vn_ref[...]
    ones = jnp.ones((B_ONES_ROWS, 3 * blk), BF16)

    def block(it, carry):
        units = []
        for j in range(B_UNROLL):
            c = it * B_UNROLL + j
            variant = _edge_variant(c, nblk, t, nt)
            rows = pl.ds(pl.multiple_of(c * blk, blk), blk)
            wrows = pl.ds(pl.multiple_of(c * blk, blk), 3 * blk)
            for kvh in range(B_KV_HEADS):
                units.append((kvh, variant, rows, wrows))
        scores = []
        for kvh, variant, rows, wrows in units:
            kcols = slice(kvh * HEAD_DIM, (kvh + 1) * HEAD_DIM)
            q4 = jnp.concatenate(
                [q_ref[rows, (kvh * B_REP + rep) * HEAD_DIM:(kvh * B_REP + rep + 1) * HEAD_DIM]
                 for rep in range(B_REP)], axis=0)
            scores.append(lax.dot_general(kw_ref[wrows, kcols], q4, (((1,), (1,)), ((), ())),
                                          preferred_element_type=F32))
        probs = []
        for (kvh, variant, rows, wrows), s in zip(units, scores):
            s = s + bias_ref[variant, kvh]
            sink = sink_ref[kvh]
            m = jnp.maximum(jnp.max(s, axis=0, keepdims=True), sink)
            probs.append((jnp.exp2(s - m).astype(BF16), jnp.exp2(sink - m)))
        for (kvh, variant, rows, wrows), (p, psink) in zip(units, probs):
            kcols = slice(kvh * HEAD_DIM, (kvh + 1) * HEAD_DIM)
            vx = jnp.concatenate([vw_ref[kcols, wrows], ones], axis=0)
            acc = jnp.dot(vx, p, preferred_element_type=F32)
            den = acc[HEAD_DIM:HEAD_DIM + 1, :] + psink
            o = (acc[:HEAD_DIM, :] / den).T.astype(o_ref.dtype)
            for rep in range(B_REP):
                h = kvh * B_REP + rep
                o_ref[rows, h * HEAD_DIM:(h + 1) * HEAD_DIM] = o[rep * blk:(rep + 1) * blk, :]
        return carry

    lax.fori_loop(0, nblk // B_UNROLL, block, 0)


def _attn_b(p_nat, vbt, bias, sink, batch, seq, tq=2048):
    m = p_nat.shape[0]
    nt = seq // tq
    assert seq % tq == 0 and nt >= 2 and tq % (B_UNROLL * B_BLOCK) == 0
    blk = B_BLOCK
    nb_tile = tq // blk
    lastb = m // blk - 1
    q_col = NAT_QB // B_Q_W
    k_col = NAT_KB // B_KV_W

    def first(b, t):
        return jnp.maximum((b * nt + t) * nb_tile - 1, 0)

    def after(b, t):
        return jnp.minimum((b * nt + t + 1) * nb_tile, lastb)

    return pl.pallas_call(
        functools.partial(_attn_b_kernel, tq=tq),
        grid=(batch, nt),
        in_specs=[
            pl.BlockSpec((tq, B_Q_W), lambda b, t: (b * nt + t, q_col)),
            pl.BlockSpec((blk, B_KV_W), lambda b, t: (first(b, t), k_col)),
            pl.BlockSpec((tq, B_KV_W), lambda b, t: (b * nt + t, k_col)),
            pl.BlockSpec((blk, B_KV_W), lambda b, t: (after(b, t), k_col)),
            pl.BlockSpec((B_KV_W, blk), lambda b, t: (0, first(b, t))),
            pl.BlockSpec((B_KV_W, tq), lambda b, t: (0, b * nt + t)),
            pl.BlockSpec((B_KV_W, blk), lambda b, t: (0, after(b, t))),
            pl.BlockSpec(bias.shape, lambda b, t: (0, 0, 0, 0)),
            pl.BlockSpec(sink.shape, lambda b, t: (0, 0, 0)),
        ],
        out_specs=pl.BlockSpec((tq, B_Q_W), lambda b, t: (b * nt + t, 0)),
        out_shape=jax.ShapeDtypeStruct((m, B_Q_W), BF16),
        scratch_shapes=[pltpu.VMEM((tq + 2 * blk, B_KV_W), BF16), pltpu.VMEM((B_KV_W, tq + 2 * blk), BF16)],
        compiler_params=_params(("arbitrary", "arbitrary")),
        name="attn_b",
    )(p_nat, p_nat, p_nat, p_nat, vbt, vbt, vbt, bias, sink)


def _mix_ln_kernel(x_ref, oa_ref, ob_ref, ga_ref, gb_ref, wpa_ref, wpb_ref, wo_ref, g_ref, b_ref, o_ref, *, alpha):
    sub = x_ref.shape[0] // MIX_SUBTILES
    ys = []
    for k in range(MIX_SUBTILES):
        rows = slice(k * sub, (k + 1) * sub)
        ya = jnp.dot(oa_ref[rows, :], wpa_ref[...], preferred_element_type=F32)
        yb = jnp.dot(ob_ref[rows, :], wpb_ref[...], preferred_element_type=F32)
        ys.append((ya, yb))
    for k, (ya, yb) in enumerate(ys):
        rows = slice(k * sub, (k + 1) * sub)
        merged = (jax.nn.sigmoid(ga_ref[rows, :].astype(F32)) * ya
                  + jax.nn.sigmoid(gb_ref[rows, :].astype(F32)) * yb)
        z = jnp.dot(merged.astype(BF16), wo_ref[...], preferred_element_type=F32)
        o_ref[rows, :] = _layer_norm(alpha * x_ref[rows, :] + z, g_ref[...], b_ref[...])


def _mix_ln(x, oa, ob, p_nat, w_pa, w_pb, w_out, ln_g, ln_b, alpha, tm=512):
    m, dm = x.shape

    def const(arr):
        return pl.BlockSpec(arr.shape, lambda i: (0, 0), pipeline_mode=pl.Buffered(1))

    return pl.pallas_call(
        functools.partial(_mix_ln_kernel, alpha=alpha),
        grid=(m // tm,),
        in_specs=[
            pl.BlockSpec((tm, dm), lambda i: (i, 0)),
            pl.BlockSpec((tm, oa.shape[1]), lambda i: (i, 0)),
            pl.BlockSpec((tm, ob.shape[1]), lambda i: (i, 0)),
            pl.BlockSpec((tm, dm), lambda i: (i, NAT_GA // dm)),
            pl.BlockSpec((tm, dm), lambda i: (i, NAT_GB // dm)),
            const(w_pa), const(w_pb), const(w_out), const(ln_g), const(ln_b),
        ],
        out_specs=pl.BlockSpec((tm, dm), lambda i: (i, 0)),
        out_shape=jax.ShapeDtypeStruct((m, dm), F32),
        compiler_params=_params(("arbitrary",)),
        name="mix_ln",
    )(x, oa, ob, p_nat, p_nat, w_pa, w_pb, w_out, ln_g, ln_b)


def _ffn_step(x_ref, wg_ref, wu_ref, wd_ref, g_ref, b_ref, o_ref, xb_ref, alpha, first, last):
    sub = x_ref.shape[0] // FFN_SUBTILES
    hs = []
    for k in range(FFN_SUBTILES):
        rows = slice(k * sub, (k + 1) * sub)
        if first:
            xb_ref[rows, :] = x_ref[rows, :].astype(BF16)
        xb = xb_ref[rows, :]
        hs.append((jnp.dot(xb, wg_ref[...], preferred_element_type=F32),
                   jnp.dot(xb, wu_ref[...], preferred_element_type=F32)))
    for k, (gate, up) in enumerate(hs):
        rows = slice(k * sub, (k + 1) * sub)
        act = (gate * jax.nn.sigmoid(gate) * up).astype(BF16)
        base = alpha * x_ref[rows, :] if first else o_ref[rows, :]
        y = base + jnp.dot(act, wd_ref[...], preferred_element_type=F32)
        o_ref[rows, :] = _layer_norm(y, g_ref[...], b_ref[...]) if last else y


def _ffn_ln_kernel(x_ref, wg_ref, wu_ref, wd_ref, g_ref, b_ref, o_ref, xb_ref, *, alpha):
    f = pl.program_id(1)
    nf = pl.num_programs(1)
    step = functools.partial(_ffn_step, x_ref, wg_ref, wu_ref, wd_ref, g_ref, b_ref, o_ref, xb_ref, alpha)
    pl.when(f == 0)(functools.partial(step, True, False))
    pl.when(jnp.logical_and(f > 0, f < nf - 1))(functools.partial(step, False, False))
    pl.when(f == nf - 1)(functools.partial(step, False, True))


def _ffn_ln(x, w_up, w_down, ln_g, ln_b, alpha, tf=512, tm=1024):
    m, dm = x.shape
    nf = w_down.shape[0] // tf
    assert m % tm == 0 and w_down.shape[0] % tf == 0 and nf >= 2 and w_up.shape[1] == 2 * w_down.shape[0]
    return pl.pallas_call(
        functools.partial(_ffn_ln_kernel, alpha=alpha),
        grid=(m // tm, nf),
        in_specs=[
            pl.BlockSpec((tm, dm), lambda i, f: (i, 0)),
            pl.BlockSpec((dm, tf), lambda i, f: (0, f)),
            pl.BlockSpec((dm, tf), lambda i, f: (0, nf + f)),
            pl.BlockSpec((tf, dm), lambda i, f: (f, 0)),
            pl.BlockSpec((1, dm), lambda i, f: (0, 0)),
            pl.BlockSpec((1, dm), lambda i, f: (0, 0)),
        ],
        out_specs=pl.BlockSpec((tm, dm), lambda i, f: (i, 0)),
        out_shape=jax.ShapeDtypeStruct((m, dm), F32),
        scratch_shapes=[pltpu.VMEM((tm, dm), BF16)],
        compiler_params=_params(("arbitrary", "arbitrary")),
        name="ffn_ln",
    )(x, w_up, w_up, w_down, ln_g, ln_b)


def _bias_tables(rel_bias):
    def toeplitz(rows, halo, dil, cols):
        width = rows + 2 * halo
        deltas = np.arange(-(halo + rows - 1), rows + halo)
        n = deltas.size
        diag = rel_bias[_t5_bucket(deltas * dil)][:, cols].T.astype(F32) * LOG2E
        skew = jnp.tile(jnp.pad(diag, ((0, 0), (0, 1))), (1, rows))[:, :rows * n].reshape(-1, rows, n)
        return skew[:, :, rows - 1:rows - 1 + width]

    bias_a = []
    for g, (window, dil) in enumerate(A_GROUPS):
        assert window // (2 * dil) == A_HALO
        heads = toeplitz(A_QROWS, A_HALO, dil, slice(g * A_HEADS_PER_GROUP, (g + 1) * A_HEADS_PER_GROUP))
        masks = _band_masks(A_QROWS, A_HALO, A_HALO)
        bias_a.append(jnp.where(masks[:, None], heads[None], NEG_INF))

    heads = toeplitz(B_BLOCK, B_BLOCK, 1, slice(A_HEADS, A_HEADS + B_Q_HEADS))
    masks = _band_masks(B_BLOCK, B_BLOCK, B_HALF)
    bias_b = jnp.where(masks[:, None], heads[None], NEG_INF)
    bias_b = bias_b.reshape(3, B_KV_HEADS, B_REP, B_BLOCK, 3 * B_BLOCK).transpose(0, 1, 4, 2, 3)
    bias_b = bias_b.reshape(3, B_KV_HEADS, 3 * B_BLOCK, B_REP * B_BLOCK)
    return bias_a, bias_b


def _prep_w_in(w_in):
    qa, ka, va = (w_in[:, i * A_QKV_W:(i + 1) * A_QKV_W] for i in range(3))
    o = 3 * A_QKV_W
    qb = w_in[:, o:o + B_Q_W]
    kb = w_in[:, o + B_Q_W:o + B_Q_W + B_KV_W]
    vb = w_in[:, o + B_Q_W + B_KV_W:o + B_Q_W + 2 * B_KV_W]
    o += B_Q_W + 2 * B_KV_W
    dm = w_in.shape[0]
    ga = w_in[:, o:o + dm]
    gb = w_in[:, o + dm:o + 2 * dm]

    def grp(w, g):
        return w[:, g * A_GROUP_W:(g + 1) * A_GROUP_W]

    w_nat = jnp.concatenate([ga, gb, qb, grp(qa, 0), grp(ka, 0), grp(va, 0), kb, vb], axis=1).astype(BF16)
    w_perm = [jnp.concatenate([grp(qa, g), grp(ka, g), grp(va, g)], axis=1).astype(BF16) for g in (1, 2)]
    s_nat = np.ones((1, NAT_W), np.float32)
    s_nat[:, NAT_QB:NAT_QB + B_Q_W] = QK_SCALE
    s_nat[:, NAT_Q0:NAT_Q0 + A_GROUP_W] = QK_SCALE
    return w_nat, w_perm, jnp.asarray(s_nat)


def kernel(x, rel_bias, w_in, sink, w_pa, w_pb, w_out, ln1_g, ln1_b, w_up, w_down, ln2_g, ln2_b):
    batch, seq, dm = x.shape
    depth = w_in.shape[0]
    assert dm == D_MODEL and w_in.shape[1:] == (dm, 3 * A_QKV_W + B_Q_W + 2 * B_KV_W + 2 * dm)
    assert rel_bias.shape == (NUM_BUCKETS, A_HEADS + B_Q_HEADS)
    alpha = (2 * depth) ** 0.25
    bias_a, bias_b = _bias_tables(rel_bias)
    h = x.reshape(batch * seq, dm)
    for l in range(depth):
        w_nat, (w4, w16), s_nat = _prep_w_in(w_in[l])
        sink_row = jnp.repeat(sink[l].reshape(B_KV_HEADS, B_REP), B_BLOCK, axis=1)[:, None, :].astype(F32) * LOG2E
        p_nat, vbt, hb = _proj_nat(h, w_nat, s_nat)
        a4, a16 = _proj_perm(hb, w4, w16, batch, seq)
        oa = _attn_a(p_nat, a4, a16, bias_a, batch, seq)
        ob = _attn_b(p_nat, vbt, bias_b, sink_row, batch, seq)
        h = _mix_ln(h, oa, ob, p_nat, w_pa[l].astype(BF16), w_pb[l].astype(BF16), w_out[l].astype(BF16),
                    ln1_g[l][None], ln1_b[l][None], alpha)
        h = _ffn_ln(h, w_up[l].astype(BF16), w_down[l].astype(BF16), ln2_g[l][None], ln2_b[l][None], alpha)
    return h.reshape(batch, seq, dm)
```
